```python
import jax, jax.numpy as jnp
from jax import lax
import numpy as np

D_MODEL = 2048
BATCH = 4
SEQ = 8192
DEPTH = 1
DEC_BATCH = 8
DEC_SEQ = 32
PAST_LEN = 4096

CHUNK = 64
N_LEFT_CHUNKS = 8
BAND = (N_LEFT_CHUNKS + 1) * CHUNK
ATT_WINDOW = N_LEFT_CHUNKS * CHUNK
MIX_WIDTH = D_MODEL
A_WIDTH = MIX_WIDTH // 2
N_HEADS_A = 8
HEAD_DIM_A = A_WIDTH // N_HEADS_A
REL_MAX = 128
B_WIDTH = MIX_WIDTH - A_WIDTH
POOL_WINDOWS = (2, 4, 8, 16)
N_POOL_GROUPS = len(POOL_WINDOWS)
POOL_GROUP = B_WIDTH // N_POOL_GROUPS
POOL_HIST = max(POOL_WINDOWS) - 1
IN_WIDTH = 3 * A_WIDTH + B_WIDTH
N_MEM = 256
N_HEADS_MEM = 4
HEAD_DIM_MEM = D_MODEL // N_HEADS_MEM
N_EXPERTS = 32
TOP_K = 4
D_FF = D_MODEL
SWIGLU_LIMIT = 7.0
SWIGLU_ALPHA = 1.702
MOE_BLOCK = 256
EPS = 1e-5
NEG_INF = -1e30

kernel_name = "hybrid_chunk_band_pool_moe_stream_step"


def rmsnorm(x, g):
    xf = x.astype(jnp.float32)
    y = xf * lax.rsqrt(jnp.mean(xf * xf, axis=-1, keepdims=True) + EPS)
    return (y * g.astype(jnp.float32)).astype(x.dtype)


def band_mask(qpos, kpos):
    qc = qpos[:, None] // CHUNK
    kc = kpos[None, :] // CHUNK
    return (kpos[None, :] >= 0) & (kc <= qc) & (kc >= qc - N_LEFT_CHUNKS)


def rel_bias(table, qpos, kpos):
    d = jnp.clip(qpos[:, None] - kpos[None, :], -REL_MAX, REL_MAX) + REL_MAX
    return table[:, d]


def band_attend(q, k, v, bias, mask):
    s = jnp.einsum('bqhd,bkhd->bhqk', q, k).astype(jnp.float32) * (HEAD_DIM_A ** -0.5)
    s = jnp.where(mask[None, None], s + bias.astype(jnp.float32)[None], NEG_INF)
    p = jax.nn.softmax(s, axis=-1).astype(v.dtype)
    return jnp.einsum('bhqk,bkhd->bqhd', p, v)


def chunk_attn_prompt(q, k, v, table):
    S = q.shape[1]
    n_chunks = S // CHUNK
    pad = ((0, 0), (ATT_WINDOW, 0), (0, 0), (0, 0))
    kp = jnp.pad(k, pad)
    vp = jnp.pad(v, pad)

    def one_chunk(c):
        start = c * CHUNK
        qc = lax.dynamic_slice_in_dim(q, start, CHUNK, axis=1)
        kb = lax.dynamic_slice_in_dim(kp, start, BAND, axis=1)
        vb = lax.dynamic_slice_in_dim(vp, start, BAND, axis=1)
        qpos = start + jnp.arange(CHUNK)
        kpos = start - ATT_WINDOW + jnp.arange(BAND)
        return band_attend(qc, kb, vb, rel_bias(table, qpos, kpos), band_mask(qpos, kpos))

    out = lax.map(one_chunk, jnp.arange(n_chunks))
    return jnp.moveaxis(out, 0, 1).reshape(q.shape)


def pool_mix(u_ext, pos0, w_pool, scale):
    T = u_ext.shape[1] - POOL_HIST
    uf = u_ext.astype(jnp.float32)
    cs = jnp.pad(jnp.cumsum(uf, axis=1), ((0, 0), (1, 0), (0, 0)))
    pos = pos0 + jnp.arange(T)
    u_new = uf[:, POOL_HIST:]
    outs = []
    for g, win_len in enumerate(POOL_WINDOWS):
        sl = slice(g * POOL_GROUP, (g + 1) * POOL_GROUP)
        wsum = cs[:, POOL_HIST + 1:POOL_HIST + 1 + T, sl] - cs[:, POOL_HIST + 1 - win_len:POOL_HIST + 1 - win_len + T, sl]
        cnt = jnp.minimum(pos + 1, win_len).astype(jnp.float32)[None, :, None]
        d = (wsum / cnt - u_new[:, :, sl]).astype(u_ext.dtype)
        outs.append(jnp.einsum('btc,cd->btd', d, w_pool[g]))
    return jnp.concatenate(outs, axis=-1) * scale


def mem_kv(mem, g, w_k, w_v):
    B, M, _ = mem.shape
    m = rmsnorm(mem, g)
    k = (m @ w_k).reshape(B, M, N_HEADS_MEM, HEAD_DIM_MEM)
    v = (m @ w_v).reshape(B, M, N_HEADS_MEM, HEAD_DIM_MEM)
    return k, v


def mem_attend(h, mk, mv, w_q, w_o):
    B, T, _ = h.shape
    q = (h @ w_q).reshape(B, T, N_HEADS_MEM, HEAD_DIM_MEM)
    s = jnp.einsum('bqhd,bkhd->bhqk', q, mk).astype(jnp.float32) * (HEAD_DIM_MEM ** -0.5)
    p = jax.nn.softmax(s, axis=-1).astype(mv.dtype)
    o = jnp.einsum('bhqk,bkhd->bqhd', p, mv).reshape(B, T, N_HEADS_MEM * HEAD_DIM_MEM)
    return o @ w_o


def moe(h, w_router, b_router, w_gu, b_gu, w_down, b_down):
    shp = h.shape
    x = h.reshape(-1, D_MODEL)
    n = x.shape[0]
    n_assign = n * TOP_K
    logits = (x @ w_router + b_router).astype(jnp.float32)
    top_l, top_e = lax.top_k(logits, TOP_K)
    gates = jax.nn.softmax(top_l, axis=-1)
    flat_e = top_e.reshape(-1)
    flat_tok = jnp.repeat(jnp.arange(n), TOP_K)
    flat_g = gates.reshape(-1)
    order = jnp.argsort(flat_e)
    e_sorted = flat_e[order]
    tok_sorted = flat_tok[order]
    g_sorted = flat_g[order]
    bs = int(min(MOE_BLOCK, max(8, n_assign // N_EXPERTS)))
    n_blocks = -(-n_assign // bs) + N_EXPERTS
    counts = jnp.bincount(flat_e, length=N_EXPERTS)
    start = jnp.cumsum(counts) - counts
    padded = (counts + bs - 1) // bs * bs
    pend = jnp.cumsum(padded)
    pstart = pend - padded
    dest = pstart[e_sorted] + (jnp.arange(n_assign) - start[e_sorted])
    xbuf = jnp.zeros((n_blocks * bs, D_MODEL), x.dtype).at[dest].set(x[tok_sorted])
    block_expert = jnp.minimum(jnp.searchsorted(pend, jnp.arange(n_blocks) * bs, side='right'), N_EXPERTS - 1)

    def expert_block(args):
        xb, e = args
        hgu = xb @ w_gu[e] + b_gu[e]
        gate = jnp.minimum(hgu[:, :D_FF], SWIGLU_LIMIT)
        up = jnp.clip(hgu[:, D_FF:], -SWIGLU_LIMIT, SWIGLU_LIMIT)
        act = (up + 1) * (gate * jax.nn.sigmoid(SWIGLU_ALPHA * gate))
        return act @ w_down[e] + b_down[e]

    ybuf = lax.map(expert_block, (xbuf.reshape(n_blocks, bs, D_MODEL), block_expert))
    y_assign = ybuf.reshape(-1, D_MODEL)[dest] * g_sorted[:, None].astype(x.dtype)
    y = jnp.zeros_like(x).at[tok_sorted].add(y_assign)
    return y.reshape(shp)


def split_heads(z):
    B, T, _ = z.shape
    q = z[..., :A_WIDTH].reshape(B, T, N_HEADS_A, HEAD_DIM_A)
    k = z[..., A_WIDTH:2 * A_WIDTH].reshape(B, T, N_HEADS_A, HEAD_DIM_A)
    v = z[..., 2 * A_WIDTH:3 * A_WIDTH].reshape(B, T, N_HEADS_A, HEAD_DIM_A)
    u = z[..., 3 * A_WIDTH:]
    return q, k, v, u


def setup_inputs(seed: int = 0) -> dict:
    key = jax.random.key(seed)
    ks = jax.random.split(key, 32)
    nrm = lambda k, shape, s: jax.random.normal(k, shape, jnp.float32) * s
    att_cache = min(ATT_WINDOW, PAST_LEN)
    return {
        "x_prompt": nrm(ks[0], (BATCH, SEQ, D_MODEL), 1.0),
        "x_sample": nrm(ks[1], (DEC_BATCH, DEC_SEQ, D_MODEL), 1.0),
        "cache_attn_k": nrm(ks[2], (DEPTH, DEC_BATCH, att_cache, N_HEADS_A, HEAD_DIM_A), 1.0),
        "cache_attn_v": nrm(ks[3], (DEPTH, DEC_BATCH, att_cache, N_HEADS_A, HEAD_DIM_A), 1.0),
        "state_pool": nrm(ks[4], (DEPTH, DEC_BATCH, POOL_HIST, B_WIDTH), 1.0),
        "cache_mem_k": nrm(ks[5], (DEPTH, DEC_BATCH, N_MEM, N_HEADS_MEM, HEAD_DIM_MEM), 1.0),
        "cache_mem_v": nrm(ks[6], (DEPTH, DEC_BATCH, N_MEM, N_HEADS_MEM, HEAD_DIM_MEM), 1.0),
        "mem_prompt": nrm(ks[7], (BATCH, N_MEM, D_MODEL), 1.0),
        "g_mix": 1.0 + nrm(ks[8], (DEPTH, D_MODEL), 0.02),
        "w_in": nrm(ks[9], (DEPTH, D_MODEL, IN_WIDTH), D_MODEL ** -0.5),
        "rel_table": nrm(ks[10], (DEPTH, N_HEADS_A, 2 * REL_MAX + 1), 0.5),
        "pool_w": nrm(ks[11], (DEPTH, N_POOL_GROUPS, POOL_GROUP, POOL_GROUP), POOL_GROUP ** -0.5),
        "pool_scale": 1.0 + nrm(ks[12], (DEPTH, B_WIDTH), 0.02),
        "w_out": nrm(ks[13], (DEPTH, MIX_WIDTH, D_MODEL), MIX_WIDTH ** -0.5),
        "g_memkv": 1.0 + nrm(ks[14], (DEPTH, D_MODEL), 0.02),
        "w_mk": nrm(ks[15], (DEPTH, D_MODEL, N_HEADS_MEM * HEAD_DIM_MEM), D_MODEL ** -0.5),
        "w_mv": nrm(ks[16], (DEPTH, D_MODEL, N_HEADS_MEM * HEAD_DIM_MEM), D_MODEL ** -0.5),
        "g_mem": 1.0 + nrm(ks[17], (DEPTH, D_MODEL), 0.02),
        "w_mq": nrm(ks[18], (DEPTH, D_MODEL, N_HEADS_MEM * HEAD_DIM_MEM), D_MODEL ** -0.5),
        "w_mo": nrm(ks[19], (DEPTH, N_HEADS_MEM * HEAD_DIM_MEM, D_MODEL), D_MODEL ** -0.5),
        "g_ffn": 1.0 + nrm(ks[20], (DEPTH, D_MODEL), 0.02),
        "w_router": nrm(ks[21], (DEPTH, D_MODEL, N_EXPERTS), D_MODEL ** -0.5),
        "b_router": nrm(ks[22], (DEPTH, N_EXPERTS), 0.01),
        "w_gu": nrm(ks[23], (DEPTH, N_EXPERTS, D_MODEL, 2 * D_FF), D_MODEL ** -0.5),
        "b_gu": nrm(ks[24], (DEPTH, N_EXPERTS, 2 * D_FF), 0.01),
        "w_down": nrm(ks[25], (DEPTH, N_EXPERTS, D_FF, D_MODEL), D_FF ** -0.5),
        "b_down": nrm(ks[26], (DEPTH, N_EXPERTS, D_MODEL), 0.01),
        "g_final": 1.0 + nrm(ks[27], (D_MODEL,), 0.02),
    }


def reference(x_prompt, x_sample, cache_attn_k, cache_attn_v, state_pool, cache_mem_k, cache_mem_v,
              mem_prompt, g_mix, w_in, rel_table, pool_w, pool_scale, w_out, g_memkv, w_mk, w_mv,
              g_mem, w_mq, w_mo, g_ffn, w_router, b_router, w_gu, b_gu, w_down, b_down, g_final):
    xp, xs = x_prompt, x_sample
    S = xp.shape[1]
    T = xs.shape[1]
    keep_p = min(ATT_WINDOW, S)
    ak_p, av_p, pool_p, mk_p, mv_p = [], [], [], [], []
    ak_s, av_s, pool_s = [], [], []
    for l in range(DEPTH):
        qa, ka, va, ub = split_heads(rmsnorm(xp, g_mix[l]) @ w_in[l])
        att = chunk_attn_prompt(qa, ka, va, rel_table[l]).reshape(xp.shape[0], S, A_WIDTH)
        u_ext = jnp.pad(ub, ((0, 0), (POOL_HIST, 0), (0, 0)))
        pool = pool_mix(u_ext, 0, pool_w[l], pool_scale[l])
        xp = xp + jnp.concatenate([att, pool], axis=-1) @ w_out[l]
        mk, mv = mem_kv(mem_prompt, g_memkv[l], w_mk[l], w_mv[l])
        xp = xp + mem_attend(rmsnorm(xp, g_mem[l]), mk, mv, w_mq[l], w_mo[l])
        xp = xp + moe(rmsnorm(xp, g_ffn[l]), w_router[l], b_router[l], w_gu[l], b_gu[l], w_down[l], b_down[l])
        ak_p.append(ka[:, S - keep_p:])
        av_p.append(va[:, S - keep_p:])
        pool_p.append(u_ext[:, -POOL_HIST:])
        mk_p.append(mk)
        mv_p.append(mv)

        qs, ks_new, vs_new, us = split_heads(rmsnorm(xs, g_mix[l]) @ w_in[l])
        W = cache_attn_k.shape[2]
        k_all = jnp.concatenate([cache_attn_k[l], ks_new], axis=1)
        v_all = jnp.concatenate([cache_attn_v[l], vs_new], axis=1)
        qpos = PAST_LEN + jnp.arange(T)
        kpos = jnp.concatenate([PAST_LEN - W + jnp.arange(W), qpos])
        att_s = band_attend(qs, k_all, v_all, rel_bias(rel_table[l], qpos, kpos), band_mask(qpos, kpos))
        att_s = att_s.reshape(xs.shape[0], T, A_WIDTH)
        us_ext = jnp.concatenate([state_pool[l], us], axis=1)
        pool_sv = pool_mix(us_ext, PAST_LEN, pool_w[l], pool_scale[l])
        xs = xs + jnp.concatenate([att_s, pool_sv], axis=-1) @ w_out[l]
        xs = xs + mem_attend(rmsnorm(xs, g_mem[l]), cache_mem_k[l], cache_mem_v[l], w_mq[l], w_mo[l])
        xs = xs + moe(rmsnorm(xs, g_ffn[l]), w_router[l], b_router[l], w_gu[l], b_gu[l], w_down[l], b_down[l])
        ak_s.append(ks_new)
        av_s.append(vs_new)
        pool_s.append(us_ext[:, -POOL_HIST:])

    y_prompt = rmsnorm(xp, g_final)
    y_sample = rmsnorm(xs, g_final)
    return (y_prompt, y_sample,
            jnp.stack(ak_p), jnp.stack(av_p), jnp.stack(pool_p), jnp.stack(mk_p), jnp.stack(mv_p),
            jnp.stack(ak_s), jnp.stack(av_s), jnp.stack(pool_s))
```

```python
import functools

import jax
import jax.numpy as jnp
from jax import lax
from jax.experimental import pallas as pl
from jax.experimental.pallas import tpu as pltpu

F32 = jnp.float32
BF16 = jnp.bfloat16
I32 = jnp.int32

D_MODEL = 2048
PAST_LEN = 4096
CHUNK = 64
N_LEFT_CHUNKS = 8
ATT_WINDOW = N_LEFT_CHUNKS * CHUNK
A_WIDTH = 1024
N_HEADS_A = 8
HEAD_DIM_A = A_WIDTH // N_HEADS_A
REL_MAX = 128
B_WIDTH = 1024
POOL_WINDOWS = (2, 4, 8, 16)
POOL_GROUP = B_WIDTH // len(POOL_WINDOWS)
POOL_HIST = max(POOL_WINDOWS) - 1
HIST_ROWS = POOL_HIST + 1
IN_WIDTH = 3 * A_WIDTH + B_WIDTH
N_MEM = 256
N_HEADS_MEM = 4
HEAD_DIM_MEM = D_MODEL // N_HEADS_MEM
N_EXPERTS = 32
TOP_K = 4
D_FF = D_MODEL
SWIGLU_LIMIT = 7.0
SWIGLU_ALPHA = 1.702
EPS = 1e-5
NEG_INF = -1e30

V7X_VMEM_LIMIT_BYTES = 56 * 1024 * 1024
LANES = 128
BIAS_SPAN = 1024


def _params(n_grid):
    return pltpu.CompilerParams(dimension_semantics=("arbitrary",) * n_grid,
                                vmem_limit_bytes=V7X_VMEM_LIMIT_BYTES)


def _rms_scale(x, g):
    return x * lax.rsqrt(jnp.mean(x * x, axis=-1, keepdims=True) + EPS) * g


def _norm_matmul_body(x_ref, g_ref, w_ref, o_ref, xn_ref):
    @pl.when(pl.program_id(1) == 0)
    def _():
        xn_ref[...] = _rms_scale(x_ref[...].astype(F32), g_ref[...]).astype(BF16)

    o_ref[...] = jnp.dot(xn_ref[...], w_ref[...], preferred_element_type=F32).astype(o_ref.dtype)


def norm_matmul(x, g, w, out_dtype, tm, tn, name):
    m, d = x.shape
    n = w.shape[1]
    tm, tn = min(tm, m), min(tn, n)
    return pl.pallas_call(
        _norm_matmul_body,
        out_shape=jax.ShapeDtypeStruct((m, n), out_dtype),
        grid=(m // tm, n // tn),
        in_specs=[pl.BlockSpec((tm, d), lambda i, j: (i, 0)),
                  pl.BlockSpec((1, d), lambda i, j: (0, 0)),
                  pl.BlockSpec((d, tn), lambda i, j: (0, j))],
        out_specs=pl.BlockSpec((tm, tn), lambda i, j: (i, j)),
        scratch_shapes=[pltpu.VMEM((tm, d), BF16)],
        compiler_params=_params(2),
        name=name,
    )(x, g.reshape(1, d), w)


def _build_band_bias(g_ref, bias_ref, tq, tk):
    row = lax.broadcasted_iota(I32, (tq, tk), 0)
    col = lax.broadcasted_iota(I32, (tq, tk), 1)
    rel_chunk = col // CHUNK - N_LEFT_CHUNKS - row // CHUNK
    in_band = (rel_chunk <= 0) & (rel_chunk >= -N_LEFT_CHUNKS)
    lo = BIAS_SPAN - ATT_WINDOW - 2 * REL_MAX
    for h in range(N_HEADS_A):
        base = jnp.broadcast_to(g_ref[h:h + 1, :], (tq, BIAS_SPAN))
        rolled = pltpu.roll(base, 0, 1, stride=1, stride_axis=0)
        bias_ref[h] = jnp.where(in_band, rolled[:, lo:lo + tk], NEG_INF)


def _band_attn_body(*refs, n_kv, tq, kw, dynamic_first):
    g_ref, q_ref = refs[0], refs[1]
    k_refs = refs[2:2 + n_kv]
    v_refs = refs[2 + n_kv:2 + 2 * n_kv]
    o_ref, bias_ref = refs[2 + 2 * n_kv], refs[3 + 2 * n_kv]
    tk = sum(kw)
    first = (pl.program_id(0) == 0) & (pl.program_id(1) == 0)

    @pl.when(first)
    def _():
        _build_band_bias(g_ref, bias_ref, tq, tk)

    scale = HEAD_DIM_A ** -0.5
    if dynamic_first:
        col = lax.broadcasted_iota(I32, (tq, tk), 1)
        exists = col >= ATT_WINDOW - pl.program_id(1) * tq
    for h in range(N_HEADS_A):
        sl = slice(h * HEAD_DIM_A, (h + 1) * HEAD_DIM_A)
        q = q_ref[:, sl]
        s = jnp.concatenate(
            [lax.dot_general(q, k_ref[:, sl].astype(BF16), (((1,), (1,)), ((), ())), preferred_element_type=F32)
             for k_ref in k_refs], axis=1)
        s = s * scale + bias_ref[h]
        if dynamic_first:
            s = jnp.where(exists, s, NEG_INF)
        m = jnp.max(s, axis=-1, keepdims=True)
        p = jnp.exp(s - m)
        l = jnp.sum(p, axis=-1, keepdims=True)
        pb = p.astype(BF16)
        o = None
        off = 0
        for v_ref, w in zip(v_refs, kw):
            part = jnp.dot(pb[:, off:off + w], v_ref[:, sl].astype(BF16), preferred_element_type=F32)
            o = part if o is None else o + part
            off += w
        o_ref[:, sl] = (o / l).astype(o_ref.dtype)


def band_attention_prompt(z, bias_rows, tq):
    b, s, _ = z.shape
    n_kv = ATT_WINDOW // tq + 1
    tk = n_kv * tq
    blk = (None, tq, A_WIDTH)

    def kv_spec(back, colblk):
        return pl.BlockSpec(blk, lambda bi, i: (bi, jnp.maximum(i - back, 0), colblk))

    in_specs = [pl.BlockSpec((N_HEADS_A, BIAS_SPAN), lambda bi, i: (0, 0)),
                pl.BlockSpec(blk, lambda bi, i: (bi, i, 0))]
    in_specs += [kv_spec(n_kv - 1 - j, 1) for j in range(n_kv)]
    in_specs += [kv_spec(n_kv - 1 - j, 2) for j in range(n_kv)]
    return pl.pallas_call(
        functools.partial(_band_attn_body, n_kv=n_kv, tq=tq, kw=(tq,) * n_kv, dynamic_first=True),
        out_shape=jax.ShapeDtypeStruct((b, s, A_WIDTH), BF16),
        grid=(b, s // tq),
        in_specs=in_specs,
        out_specs=pl.BlockSpec(blk, lambda bi, i: (bi, i, 0)),
        scratch_shapes=[pltpu.VMEM((N_HEADS_A, tq, tk), F32)],
        compiler_params=_params(2),
        name="band_attn_prompt",
    )(bias_rows, *([z] * (1 + 2 * n_kv)))


def band_attention_sample(z, cache_k, cache_v, bias_rows):
    b, t, _ = z.shape
    w = cache_k.shape[1]
    assert w == ATT_WINDOW and PAST_LEN >= w and PAST_LEN % CHUNK == 0 and t <= CHUNK
    new_blk = (None, t, A_WIDTH)
    old_blk = (None, w, A_WIDTH)
    return pl.pallas_call(
        functools.partial(_band_attn_body, n_kv=2, tq=t, kw=(w, t), dynamic_first=False),
        out_shape=jax.ShapeDtypeStruct((b, t, A_WIDTH), BF16),
        grid=(b, 1),
        in_specs=[pl.BlockSpec((N_HEADS_A, BIAS_SPAN), lambda bi, i: (0, 0)),
                  pl.BlockSpec(new_blk, lambda bi, i: (bi, 0, 0)),
                  pl.BlockSpec(old_blk, lambda bi, i: (bi, 0, 0)),
                  pl.BlockSpec(new_blk, lambda bi, i: (bi, 0, 1)),
                  pl.BlockSpec(old_blk, lambda bi, i: (bi, 0, 0)),
                  pl.BlockSpec(new_blk, lambda bi, i: (bi, 0, 2))],
        out_specs=pl.BlockSpec(new_blk, lambda bi, i: (bi, 0, 0)),
        scratch_shapes=[pltpu.VMEM((N_HEADS_A, t, w + t), F32)],
        compiler_params=_params(2),
        name="band_attn_sample",
    )(bias_rows, z, cache_k, z, cache_v, z)


def _outproj_body(att_ref, u_ref, hist_ref, x_ref, wp_ref, ps_ref, w_ref, o_ref, cat_ref, *, pos0, tm,
                  zero_first_hist):
    i = pl.program_id(1)

    @pl.when(pl.program_id(2) == 0)
    def _():
        cat_ref[:, :A_WIDTH] = att_ref[...]
        hist = hist_ref[...].astype(F32)
        if zero_first_hist:
            hist = jnp.where(i > 0, hist, 0.0)
        ext = jnp.concatenate([hist, u_ref[...].astype(F32)], axis=0)
        pos = pos0 + i * tm + lax.broadcasted_iota(I32, (tm, 1), 0)
        for g, win in enumerate(POOL_WINDOWS):
            sl = slice(g * POOL_GROUP, (g + 1) * POOL_GROUP)
            e = ext[:, sl]
            run, span = e, 1
            while span < win:
                run = run[span:] + run[:-span]
                span *= 2
            first = HIST_ROWS - win + 1
            wsum = run[first:first + tm]
            cnt = jnp.minimum(pos + 1, win).astype(F32)
            d = (wsum / cnt - e[HIST_ROWS:]).astype(BF16)
            pool = jnp.dot(d, wp_ref[g], preferred_element_type=F32) * ps_ref[:, sl]
            cat_ref[:, A_WIDTH + g * POOL_GROUP:A_WIDTH + (g + 1) * POOL_GROUP] = pool.astype(BF16)

    o_ref[...] = x_ref[...] + jnp.dot(cat_ref[...], w_ref[...], preferred_element_type=F32)


def outproj(att, u_src, u_colblk, hist_src, hist_map, x, pool_w, pool_scale, w_out, pos0, tm, tn,
            zero_first_hist, name):
    b, s, d = x.shape
    tm = min(tm, s)
    return pl.pallas_call(
        functools.partial(_outproj_body, pos0=pos0, tm=tm, zero_first_hist=zero_first_hist),
        out_shape=jax.ShapeDtypeStruct((b, s, d), F32),
        grid=(b, s // tm, d // tn),
        in_specs=[pl.BlockSpec((None, tm, A_WIDTH), lambda bi, i, j: (bi, i, 0)),
                  pl.BlockSpec((None, tm, B_WIDTH), lambda bi, i, j: (bi, i, u_colblk)),
                  pl.BlockSpec((None, HIST_ROWS, B_WIDTH), hist_map),
                  pl.BlockSpec((None, tm, tn), lambda bi, i, j: (bi, i, j)),
                  pl.BlockSpec((len(POOL_WINDOWS), POOL_GROUP, POOL_GROUP), lambda bi, i, j: (0, 0, 0)),
                  pl.BlockSpec((1, B_WIDTH), lambda bi, i, j: (0, 0)),
                  pl.BlockSpec((A_WIDTH + B_WIDTH, tn), lambda bi, i, j: (0, j))],
        out_specs=pl.BlockSpec((None, tm, tn), lambda bi, i, j: (bi, i, j)),
        scratch_shapes=[pltpu.VMEM((tm, A_WIDTH + B_WIDTH), BF16)],
        compiler_params=_params(3),
        name=name,
    )(att, u_src, hist_src, x, pool_w, pool_scale.reshape(1, B_WIDTH), w_out)


def _memattn_body(q_ref, mk_ref, mv_ref, x_ref, w_ref, o_ref, oh_ref):
    @pl.when(pl.program_id(2) == 0)
    def _():
        scale = HEAD_DIM_MEM ** -0.5
        for h in range(N_HEADS_MEM):
            sl = slice(h * HEAD_DIM_MEM, (h + 1) * HEAD_DIM_MEM)
            s = lax.dot_general(q_ref[:, sl], mk_ref[:, sl].astype(BF16), (((1,), (1,)), ((), ())),
                                preferred_element_type=F32) * scale
            p = jnp.exp(s - jnp.max(s, axis=-1, keepdims=True))
            l = jnp.sum(p, axis=-1, keepdims=True)
            o = jnp.dot(p.astype(BF16), mv_ref[:, sl].astype(BF16), preferred_element_type=F32) / l
            oh_ref[:, sl] = o.astype(BF16)

    o_ref[...] = x_ref[...] + jnp.dot(oh_ref[...], w_ref[...], preferred_element_type=F32)


def mem_attention(q, mk, mv, x, w_mo, tm, tn, name):
    b, s, d = x.shape
    tm = min(tm, s)
    return pl.pallas_call(
        _memattn_body,
        out_shape=jax.ShapeDtypeStruct((b, s, d), F32),
        grid=(b, s // tm, d // tn),
        in_specs=[pl.BlockSpec((None, tm, d), lambda bi, i, j: (bi, i, 0)),
                  pl.BlockSpec((None, N_MEM, d), lambda bi, i, j: (bi, 0, 0)),
                  pl.BlockSpec((None, N_MEM, d), lambda bi, i, j: (bi, 0, 0)),
                  pl.BlockSpec((None, tm, tn), lambda bi, i, j: (bi, i, j)),
                  pl.BlockSpec((d, tn), lambda bi, i, j: (0, j))],
        out_specs=pl.BlockSpec((None, tm, tn), lambda bi, i, j: (bi, i, j)),
        scratch_shapes=[pltpu.VMEM((tm, d), BF16)],
        compiler_params=_params(3),
        name=name,
    )(q, mk, mv, x, w_mo)


def _router_body(x_ref, g_ref, w_ref, b_ref, xn_ref, ids_ref, gates_ref, rank_ref, cnt_ref, carry_ref, *, tm):
    @pl.when(pl.program_id(0) == 0)
    def _():
        carry_ref[...] = jnp.zeros_like(carry_ref)

    xn = _rms_scale(x_ref[...], g_ref[...])
    xn_ref[...] = xn
    x_hi = xn.astype(BF16)
    x_lo = (xn - x_hi.astype(F32)).astype(BF16)
    w = w_ref[...]
    w_hi = w.astype(BF16)
    w_lo = (w - w_hi.astype(F32)).astype(BF16)
    logits = (jnp.dot(x_hi, w_hi, preferred_element_type=F32) + jnp.dot(x_lo, w_hi, preferred_element_type=F32)
              + jnp.dot(x_hi, w_lo, preferred_element_type=F32)) + b_ref[...]

    lane = lax.broadcasted_iota(I32, (tm, N_EXPERTS), 1)
    kcol = lax.broadcasted_iota(I32, (tm, TOP_K), 1)
    work = logits
    multi_hot = jnp.zeros((tm, N_EXPERTS), F32)
    ids = jnp.zeros((tm, TOP_K), I32)
    tops = jnp.zeros((tm, TOP_K), F32)
    picks = []
    for k in range(TOP_K):
        best = jnp.max(work, axis=-1, keepdims=True)
        idx = jnp.min(jnp.where(work == best, lane, N_EXPERTS), axis=-1, keepdims=True)
        hit = lane == idx
        picks.append(hit)
        multi_hot = multi_hot + hit.astype(F32)
        ids = jnp.where(kcol == k, idx, ids)
        tops = jnp.where(kcol == k, best, tops)
        work = jnp.where(hit, -jnp.inf, work)
    e = jnp.exp(tops - tops[:, 0:1])
    gates_ref[...] = e / jnp.sum(e, axis=-1, keepdims=True)
    ids_ref[...] = ids

    r = lax.broadcasted_iota(I32, (tm, tm), 0)
    c = lax.broadcasted_iota(I32, (tm, tm), 1)
    earlier = (c < r).astype(BF16)
    before = jnp.dot(earlier, multi_hot.astype(BF16), preferred_element_type=F32) + carry_ref[...]
    rank = jnp.zeros((tm, TOP_K), F32)
    for k in range(TOP_K):
        rk = jnp.sum(jnp.where(picks[k], before, 0.0), axis=-1, keepdims=True)
        rank = jnp.where(kcol == k, rk, rank)
    rank_ref[...] = rank.astype(I32)
    total = carry_ref[...] + jnp.sum(multi_hot, axis=0, keepdims=True)
    carry_ref[...] = total
    cnt_ref[...] = total.astype(I32)


def router(x, g, w_router, b_router, tm, name):
    n, d = x.shape
    tm = min(tm, n)
    row = lambda i: (i, 0)
    fixed = lambda i: (0, 0)
    return pl.pallas_call(
        functools.partial(_router_body, tm=tm),
        out_shape=(jax.ShapeDtypeStruct((n, d), F32),
                   jax.ShapeDtypeStruct((n, TOP_K), I32),
                   jax.ShapeDtypeStruct((n, TOP_K), F32),
                   jax.ShapeDtypeStruct((n, TOP_K), I32),
                   jax.ShapeDtypeStruct((1, N_EXPERTS), I32)),
        grid=(n // tm,),
        in_specs=[pl.BlockSpec((tm, d), row), pl.BlockSpec((1, d), fixed),
                  pl.BlockSpec((d, N_EXPERTS), fixed), pl.BlockSpec((1, N_EXPERTS), fixed)],
        out_specs=(pl.BlockSpec((tm, d), row), pl.BlockSpec((tm, TOP_K), row), pl.BlockSpec((tm, TOP_K), row),
                   pl.BlockSpec((tm, TOP_K), row), pl.BlockSpec((1, N_EXPERTS), fixed)),
        scratch_shapes=[pltpu.VMEM((1, N_EXPERTS), F32)],
        compiler_params=_params(1),
        name=name,
    )(x, g.reshape(1, d), w_router, b_router.reshape(1, N_EXPERTS))


def _dest_body(pstart_ref, ids_ref, rank_ref, dest_ref):
    ids = ids_ref[...]
    dest = rank_ref[...]
    for e in range(N_EXPERTS):
        dest = dest + jnp.where(ids == e, pstart_ref[e], 0)
    dest_ref[...] = dest


def assignment_dest(pstart, ids, rank, name):
    n_assign = ids.size
    shape = (n_assign // LANES, LANES)
    spec = pl.BlockSpec(shape, lambda i, ps: (0, 0))
    dest = pl.pallas_call(
        _dest_body,
        out_shape=jax.ShapeDtypeStruct(shape, I32),
        grid_spec=pltpu.PrefetchScalarGridSpec(num_scalar_prefetch=1, grid=(1,), in_specs=[spec, spec],
                                               out_specs=spec),
        compiler_params=_params(1),
        name=name,
    )(pstart, ids.reshape(shape), rank.reshape(shape))
    return dest.reshape(n_assign)


def _dispatch_body(fill_ref, cnt_ref, dest_ref, x_hbm, xbuf_hbm, zero_ref, sem, zsem, *, chunk, bs):
    step = pl.program_id(0)
    tok0 = step * (chunk // TOP_K)

    def row_copy(tok, dst):
        return pltpu.make_async_copy(x_hbm.at[pl.ds(tok, 1)], xbuf_hbm.at[pl.ds(dst, 1)], sem)

    def issue(a, carry):
        row_copy(tok0 + a // TOP_K, dest_ref[0, 0, a]).start()
        return carry

    lax.fori_loop(0, chunk, issue, 0)

    @pl.when(step == 0)
    def _():
        zero_ref[...] = jnp.zeros_like(zero_ref)

        def pad_copy(dst):
            return pltpu.make_async_copy(zero_ref, xbuf_hbm.at[pl.ds(dst, 1)], zsem)

        def per_expert(e, carry):
            start = fill_ref[e]
            n_pad = cnt_ref[e]

            def one(r, c2):
                pad_copy(start + r).start()
                return c2

            lax.fori_loop(0, n_pad, one, 0)

            def one_wait(r, c2):
                pad_copy(start + r).wait()
                return c2

            lax.fori_loop(0, n_pad, one_wait, 0)
            return carry

        lax.fori_loop(0, N_EXPERTS, per_expert, 0)

    def drain(a, carry):
        row_copy(tok0 + a // TOP_K, dest_ref[0, 0, a]).wait()
        return carry

    lax.fori_loop(0, chunk, drain, 0)


def dispatch(xn, dest, pad_start, pad_count, n_rows, bs, chunk, name):
    n, d = xn.shape
    n_assign = n * TOP_K
    chunk = min(chunk, n_assign)
    return pl.pallas_call(
        functools.partial(_dispatch_body, chunk=chunk, bs=bs),
        out_shape=jax.ShapeDtypeStruct((n_rows, d), xn.dtype),
        grid_spec=pltpu.PrefetchScalarGridSpec(
            num_scalar_prefetch=2,
            grid=(n_assign // chunk,),
            in_specs=[pl.BlockSpec((1, 1, chunk), lambda i, a, b: (i, 0, 0), memory_space=pltpu.SMEM),
                      pl.BlockSpec(memory_space=pl.ANY)],
            out_specs=pl.BlockSpec(memory_space=pl.ANY),
            scratch_shapes=[pltpu.VMEM((1, d), xn.dtype), pltpu.SemaphoreType.DMA, pltpu.SemaphoreType.DMA]),
        compiler_params=_params(1),
        name=name,
    )(pad_start, pad_count, dest.reshape(n_assign // chunk, 1, chunk), xn)


def _expert_body(be_ref, nu_ref, x_ref, wg_ref, wu_ref, bg_ref, bu_ref, wd_ref, bd_ref, o_ref, xb_ref, acc_ref):
    j = pl.program_id(1)

    @pl.when(pl.program_id(0) < nu_ref[0])
    def _():
        @pl.when(j == 0)
        def _():
            xb_ref[...] = x_ref[...].astype(BF16)
            acc_ref[...] = jnp.broadcast_to(bd_ref[...], acc_ref.shape)

        xb = xb_ref[...]
        gate = jnp.dot(xb, wg_ref[...], preferred_element_type=F32) + bg_ref[...]
        up = jnp.dot(xb, wu_ref[...], preferred_element_type=F32) + bu_ref[...]
        gate = jnp.minimum(gate, SWIGLU_LIMIT)
        up = jnp.clip(up, -SWIGLU_LIMIT, SWIGLU_LIMIT)
        act = (up + 1.0) * (gate * jax.nn.sigmoid(SWIGLU_ALPHA * gate))
        acc_ref[...] += jnp.dot(act.astype(BF16), wd_ref[...], preferred_element_type=F32)

        @pl.when(j == pl.num_programs(1) - 1)
        def _():
            o_ref[...] = acc_ref[...]


def expert_ffn(block_expert, n_used, xbuf, w_gu, b_gu, w_down, b_down, bs, tf, name):
    n_rows, d = xbuf.shape
    n_blocks = n_rows // bs
    nt = D_FF // tf

    def blk(b, nu):
        return jnp.minimum(b, nu[0] - 1)

    def ff(b, j, nu):
        return jnp.where(b < nu[0], j, nt - 1)

    return pl.pallas_call(
        _expert_body,
        out_shape=jax.ShapeDtypeStruct((n_rows, d), F32),
        grid_spec=pltpu.PrefetchScalarGridSpec(
            num_scalar_prefetch=2,
            grid=(n_blocks, nt),
            in_specs=[pl.BlockSpec((bs, d), lambda b, j, be, nu: (blk(b, nu), 0)),
                      pl.BlockSpec((None, d, tf), lambda b, j, be, nu: (be[blk(b, nu)], 0, ff(b, j, nu))),
                      pl.BlockSpec((None, d, tf), lambda b, j, be, nu: (be[blk(b, nu)], 0, nt + ff(b, j, nu))),
                      pl.BlockSpec((None, 1, tf), lambda b, j, be, nu: (be[blk(b, nu)], 0, ff(b, j, nu))),
                      pl.BlockSpec((None, 1, tf), lambda b, j, be, nu: (be[blk(b, nu)], 0, nt + ff(b, j, nu))),
                      pl.BlockSpec((None, tf, d), lambda b, j, be, nu: (be[blk(b, nu)], ff(b, j, nu), 0)),
                      pl.BlockSpec((None, 1, d), lambda b, j, be, nu: (be[blk(b, nu)], 0, 0))],
            out_specs=pl.BlockSpec((bs, d), lambda b, j, be, nu: (blk(b, nu), 0)),
            scratch_shapes=[pltpu.VMEM((bs, d), BF16), pltpu.VMEM((bs, d), F32)]),
        compiler_params=_params(2),
        name=name,
    )(block_expert, n_used, xbuf, w_gu, w_gu, b_gu.reshape(N_EXPERTS, 1, 2 * D_FF),
      b_gu.reshape(N_EXPERTS, 1, 2 * D_FF), w_down, b_down.reshape(N_EXPERTS, 1, d))


def _combine_body(dest_ref, gates_ref, h_ref, g_ref, ybuf_hbm, o_ref, buf_ref, sem, *, tc):
    def row_copy(a):
        return pltpu.make_async_copy(ybuf_hbm.at[pl.ds(dest_ref[0, 0, a], 1)],
                                     buf_ref.at[a % TOP_K, pl.ds(a // TOP_K, 1)], sem)

    def issue(a, carry):
        row_copy(a).start()
        return carry

    lax.fori_loop(0, tc * TOP_K, issue, 0)

    def drain(a, carry):
        row_copy(a).wait()
        return carry

    lax.fori_loop(0, tc * TOP_K, drain, 0)

    gates = gates_ref[...]
    y = h_ref[...]
    for k in range(TOP_K):
        y = y + buf_ref[k] * gates[:, k:k + 1]
    o_ref[...] = _rms_scale(y, g_ref[...])


def combine(dest, gates, h, g_final, ybuf, tc, name):
    n, d = h.shape
    tc = min(tc, n)
    return pl.pallas_call(
        functools.partial(_combine_body, tc=tc),
        out_shape=jax.ShapeDtypeStruct((n, d), F32),
        grid=(n // tc,),
        in_specs=[pl.BlockSpec((1, 1, tc * TOP_K), lambda i: (i, 0, 0), memory_space=pltpu.SMEM),
                  pl.BlockSpec((tc, TOP_K), lambda i: (i, 0)),
                  pl.BlockSpec((tc, d), lambda i: (i, 0)),
                  pl.BlockSpec((1, d), lambda i: (0, 0)),
                  pl.BlockSpec(memory_space=pl.ANY)],
        out_specs=pl.BlockSpec((tc, d), lambda i: (i, 0)),
        scratch_shapes=[pltpu.VMEM((TOP_K, tc, d), F32), pltpu.SemaphoreType.DMA],
        compiler_params=_params(1),
        name=name,
    )(dest.reshape(n // tc, 1, tc * TOP_K), gates, h, g_final.reshape(1, d), ybuf)


def moe_and_final_norm(h, g_ffn, w_router, b_router, w_gu, b_gu, w_down, b_down, g_final, *, router_tm, bs, tf,
                       dispatch_chunk, combine_tc, tag):
    n, d = h.shape
    n_assign = n * TOP_K
    n_blocks = n_assign // bs + N_EXPERTS
    xn, ids, gates, rank, counts = router(h, g_ffn, w_router, b_router, router_tm, "router_" + tag)
    counts = counts.reshape(N_EXPERTS)
    padded = (counts + bs - 1) // bs * bs
    pend = jnp.cumsum(padded)
    pstart = pend - padded
    n_used = (pend[-1:] // bs).astype(I32)
    block_expert = jnp.minimum(jnp.searchsorted(pend, jnp.arange(n_blocks, dtype=I32) * bs, side="right"),
                               N_EXPERTS - 1).astype(I32)
    dest = assignment_dest(pstart.astype(I32), ids, rank, "dest_" + tag)
    xbuf = dispatch(xn, dest, (pstart + counts).astype(I32), (padded - counts).astype(I32), n_blocks * bs, bs,
                    dispatch_chunk, "dispatch_" + tag)
    ybuf = expert_ffn(block_expert, n_used, xbuf, w_gu, b_gu, w_down, b_down, bs, tf, "experts_" + tag)
    return combine(dest, gates, h, g_final, ybuf, combine_tc, "combine_" + tag)


def _bias_rows(table):
    flipped = table[:, ::-1]
    left = BIAS_SPAN - 3 * REL_MAX
    right = BIAS_SPAN - left - (2 * REL_MAX + 1)
    return jnp.pad(flipped, ((0, 0), (left, right)), mode="edge").astype(F32)


def kernel(x_prompt, x_sample, cache_attn_k, cache_attn_v, state_pool, cache_mem_k, cache_mem_v, mem_prompt, g_mix,
           w_in, rel_table, pool_w, pool_scale, w_out, g_memkv, w_mk, w_mv, g_mem, w_mq, w_mo, g_ffn, w_router,
           b_router, w_gu, b_gu, w_down, b_down, g_final):
    bp, s, d = x_prompt.shape
    bsz, t, _ = x_sample.shape
    l = 0
    keep = min(ATT_WINDOW, s)

    w_in_b = w_in[l].astype(BF16)
    w_out_b = w_out[l].astype(BF16)
    w_mk_b, w_mv_b = w_mk[l].astype(BF16), w_mv[l].astype(BF16)
    w_mq_b, w_mo_b = w_mq[l].astype(BF16), w_mo[l].astype(BF16)
    pool_w_b = pool_w[l].astype(BF16)
    w_gu_b, w_down_b = w_gu[l].astype(BF16), w_down[l].astype(BF16)
    bias_rows = _bias_rows(rel_table[l])
    experts = (g_ffn[l], w_router[l], b_router[l], w_gu_b, b_gu[l], w_down_b, b_down[l], g_final)

    z = norm_matmul(x_prompt.reshape(bp * s, d), g_mix[l], w_in_b, BF16, 1024, 1024, "in_proj_p")
    z = z.reshape(bp, s, IN_WIDTH)
    att = band_attention_prompt(z, bias_rows, 256)
    tm_o = 512
    h = outproj(att, z, 3, z, lambda bi, i, j: (bi, jnp.maximum(i * (tm_o // HIST_ROWS) - 1, 0), 3),
                x_prompt, pool_w_b, pool_scale[l], w_out_b, 0, tm_o, 1024, True, "outproj_p")
    mem2d = mem_prompt.reshape(bp * N_MEM, d)
    mk = norm_matmul(mem2d, g_memkv[l], w_mk_b, F32, 1024, 1024, "mem_k_p").reshape(bp, N_MEM, d)
    mv = norm_matmul(mem2d, g_memkv[l], w_mv_b, F32, 1024, 1024, "mem_v_p").reshape(bp, N_MEM, d)
    qm = norm_matmul(h.reshape(bp * s, d), g_mem[l], w_mq_b, BF16, 1024, 1024, "mem_q_p").reshape(bp, s, d)
    h = mem_attention(qm, mk, mv, h, w_mo_b, 512, 1024, "mem_attn_p")
    y_prompt = moe_and_final_norm(h.reshape(bp * s, d), *experts, router_tm=512, bs=512, tf=512,
                                  dispatch_chunk=2048, combine_tc=256, tag="p").reshape(bp, s, d)

    new_k_p = z[:, s - keep:, A_WIDTH:2 * A_WIDTH].astype(F32).reshape(1, bp, keep, N_HEADS_A, HEAD_DIM_A)
    new_v_p = z[:, s - keep:, 2 * A_WIDTH:3 * A_WIDTH].astype(F32).reshape(1, bp, keep, N_HEADS_A, HEAD_DIM_A)
    new_pool_p = z[:, s - POOL_HIST:, 3 * A_WIDTH:].astype(F32)[None]
    new_mk_p = mk.reshape(1, bp, N_MEM, N_HEADS_MEM, HEAD_DIM_MEM)
    new_mv_p = mv.reshape(1, bp, N_MEM, N_HEADS_MEM, HEAD_DIM_MEM)

    zs = norm_matmul(x_sample.reshape(bsz * t, d), g_mix[l], w_in_b, BF16, 256, 1024, "in_proj_s")
    zs = zs.reshape(bsz, t, IN_WIDTH)
    w_cache = cache_attn_k.shape[2]
    att_s = band_attention_sample(zs, cache_attn_k[l].reshape(bsz, w_cache, A_WIDTH),
                                  cache_attn_v[l].reshape(bsz, w_cache, A_WIDTH), bias_rows)
    hist_s = jnp.pad(state_pool[l], ((0, 0), (HIST_ROWS - POOL_HIST, 0), (0, 0)))
    hs = outproj(att_s, zs, 3, hist_s, lambda bi, i, j: (bi, 0, 0), x_sample, pool_w_b, pool_scale[l], w_out_b,
                 PAST_LEN, t, 1024, False, "outproj_s")
    qs = norm_matmul(hs.reshape(bsz * t, d), g_mem[l], w_mq_b, BF16, 256, 1024, "mem_q_s").reshape(bsz, t, d)
    hs = mem_attention(qs, cache_mem_k[l].reshape(bsz, N_MEM, d), cache_mem_v[l].reshape(bsz, N_MEM, d), hs,
                       w_mo_b, t, 1024, "mem_attn_s")
    y_sample = moe_and_final_norm(hs.reshape(bsz * t, d), *experts, router_tm=256, bs=128, tf=512,
                                  dispatch_chunk=1024, combine_tc=256, tag="s").reshape(bsz, t, d)

    new_k_s = zs[:, :, A_WIDTH:2 * A_WIDTH].astype(F32).reshape(1, bsz, t, N_HEADS_A, HEAD_DIM_A)
    new_v_s = zs[:, :, 2 * A_WIDTH:3 * A_WIDTH].astype(F32).reshape(1, bsz, t, N_HEADS_A, HEAD_DIM_A)
    us_ext = jnp.concatenate([state_pool[l], zs[:, :, 3 * A_WIDTH:].astype(F32)], axis=1)
    new_pool_s = us_ext[:, -POOL_HIST:][None]

    return (y_prompt, y_sample, new_k_p, new_v_p, new_pool_p, new_mk_p, new_mv_p, new_k_s, new_v_s, new_pool_s)
```

```python
import functools

import jax
import jax.numpy as jnp
from jax import lax
from jax.experimental import pallas as pl
from jax.experimental.pallas import tpu as pltpu

F32 = jnp.float32
BF16 = jnp.bfloat16
I32 = jnp.int32

D_MODEL = 2048
PAST_LEN = 4096
CHUNK = 64
N_LEFT_CHUNKS = 8
ATT_WINDOW = N_LEFT_CHUNKS * CHUNK
A_WIDTH = 1024
N_HEADS_A = 8
HEAD_DIM_A = A_WIDTH // N_HEADS_A
REL_MAX = 128
B_WIDTH = 1024
POOL_WINDOWS = (2, 4, 8, 16)
POOL_GROUP = B_WIDTH // len(POOL_WINDOWS)
POOL_HIST = max(POOL_WINDOWS) - 1
HIST_ROWS = POOL_HIST + 1
IN_WIDTH = 3 * A_WIDTH + B_WIDTH
N_MEM = 256
N_HEADS_MEM = 4
HEAD_DIM_MEM = D_MODEL // N_HEADS_MEM
N_EXPERTS = 32
TOP_K = 4
D_FF = D_MODEL
SWIGLU_LIMIT = 7.0
SWIGLU_ALPHA = 1.702
EPS = 1e-5
NEG_INF = -1e30

V7X_VMEM_LIMIT_BYTES = 56 * 1024 * 1024
LANES = 128
BIAS_SPAN = 1024
ZERO_ROWS = 128


def _params(n_grid):
    return pltpu.CompilerParams(dimension_semantics=("arbitrary",) * n_grid,
                                vmem_limit_bytes=V7X_VMEM_LIMIT_BYTES)


def _rms_scale(x, g):
    return x * lax.rsqrt(jnp.mean(x * x, axis=-1, keepdims=True) + EPS) * g


def _norm_matmul_body(x_ref, g_ref, w_ref, o_ref, xn_ref):
    @pl.when(pl.program_id(1) == 0)
    def _():
        xn_ref[...] = _rms_scale(x_ref[...].astype(F32), g_ref[...]).astype(BF16)

    o_ref[...] = jnp.dot(xn_ref[...], w_ref[...], preferred_element_type=F32).astype(o_ref.dtype)


def norm_matmul(x, g, w, out_dtype, tm, tn, name):
    m, d = x.shape
    n = w.shape[1]
    tm, tn = min(tm, m), min(tn, n)
    return pl.pallas_call(
        _norm_matmul_body,
        out_shape=jax.ShapeDtypeStruct((m, n), out_dtype),
        grid=(m // tm, n // tn),
        in_specs=[pl.BlockSpec((tm, d), lambda i, j: (i, 0)),
                  pl.BlockSpec((1, d), lambda i, j: (0, 0)),
                  pl.BlockSpec((d, tn), lambda i, j: (0, j))],
        out_specs=pl.BlockSpec((tm, tn), lambda i, j: (i, j)),
        scratch_shapes=[pltpu.VMEM((tm, d), BF16)],
        compiler_params=_params(2),
        name=name,
    )(x, g.reshape(1, d), w)


def _build_band_bias(g_ref, bias_ref, tq, tk):
    row = lax.broadcasted_iota(I32, (tq, tk), 0)
    col = lax.broadcasted_iota(I32, (tq, tk), 1)
    rel_chunk = col // CHUNK - N_LEFT_CHUNKS - row // CHUNK
    in_band = (rel_chunk <= 0) & (rel_chunk >= -N_LEFT_CHUNKS)
    lo = BIAS_SPAN - ATT_WINDOW - 2 * REL_MAX
    for h in range(N_HEADS_A):
        base = jnp.broadcast_to(g_ref[h:h + 1, :], (tq, BIAS_SPAN))
        rolled = pltpu.roll(base, 0, 1, stride=1, stride_axis=0)
        bias_ref[h] = jnp.where(in_band, rolled[:, lo:lo + tk], NEG_INF)


def _band_attn_body(*refs, n_kv, tq, kw, dynamic_first):
    g_ref, q_ref = refs[0], refs[1]
    k_refs = refs[2:2 + n_kv]
    v_refs = refs[2 + n_kv:2 + 2 * n_kv]
    o_ref, bias_ref = refs[2 + 2 * n_kv], refs[3 + 2 * n_kv]
    tk = sum(kw)
    first = (pl.program_id(0) == 0) & (pl.program_id(1) == 0)

    @pl.when(first)
    def _():
        _build_band_bias(g_ref, bias_ref, tq, tk)

    scale = HEAD_DIM_A ** -0.5
    if dynamic_first:
        col = lax.broadcasted_iota(I32, (tq, tk), 1)
        exists = col >= ATT_WINDOW - pl.program_id(1) * tq
    for h in range(N_HEADS_A):
        sl = slice(h * HEAD_DIM_A, (h + 1) * HEAD_DIM_A)
        q = q_ref[:, sl]
        s = jnp.concatenate(
            [lax.dot_general(q, k_ref[:, sl].astype(BF16), (((1,), (1,)), ((), ())), preferred_element_type=F32)
             for k_ref in k_refs], axis=1)
        s = s * scale + bias_ref[h]
        if dynamic_first:
            s = jnp.where(exists, s, NEG_INF)
        m = jnp.max(s, axis=-1, keepdims=True)
        p = jnp.exp(s - m)
        l = jnp.sum(p, axis=-1, keepdims=True)
        pb = p.astype(BF16)
        o = None
        off = 0
        for v_ref, w in zip(v_refs, kw):
            part = jnp.dot(pb[:, off:off + w], v_ref[:, sl].astype(BF16), preferred_element_type=F32)
            o = part if o is None else o + part
            off += w
        o_ref[:, sl] = (o / l).astype(o_ref.dtype)


def band_attention_prompt(z, bias_rows, tq):
    b, s, _ = z.shape
    n_kv = ATT_WINDOW // tq + 1
    tk = n_kv * tq
    blk = (None, tq, A_WIDTH)

    def kv_spec(back, colblk):
        return pl.BlockSpec(blk, lambda bi, i: (bi, jnp.maximum(i - back, 0), colblk))

    in_specs = [pl.BlockSpec((N_HEADS_A, BIAS_SPAN), lambda bi, i: (0, 0)),
                pl.BlockSpec(blk, lambda bi, i: (bi, i, 0))]
    in_specs += [kv_spec(n_kv - 1 - j, 1) for j in range(n_kv)]
    in_specs += [kv_spec(n_kv - 1 - j, 2) for j in range(n_kv)]
    return pl.pallas_call(
        functools.partial(_band_attn_body, n_kv=n_kv, tq=tq, kw=(tq,) * n_kv, dynamic_first=True),
        out_shape=jax.ShapeDtypeStruct((b, s, A_WIDTH), BF16),
        grid=(b, s // tq),
        in_specs=in_specs,
        out_specs=pl.BlockSpec(blk, lambda bi, i: (bi, i, 0)),
        scratch_shapes=[pltpu.VMEM((N_HEADS_A, tq, tk), F32)],
        compiler_params=_params(2),
        name="band_attn_prompt",
    )(bias_rows, *([z] * (1 + 2 * n_kv)))


def band_attention_sample(z, cache_k, cache_v, bias_rows):
    b, t, _ = z.shape
    w = cache_k.shape[1]
    assert w == ATT_WINDOW and PAST_LEN >= w and PAST_LEN % CHUNK == 0 and t <= CHUNK
    new_blk = (None, t, A_WIDTH)
    old_blk = (None, w, A_WIDTH)
    return pl.pallas_call(
        functools.partial(_band_attn_body, n_kv=2, tq=t, kw=(w, t), dynamic_first=False),
        out_shape=jax.ShapeDtypeStruct((b, t, A_WIDTH), BF16),
        grid=(b, 1),
        in_specs=[pl.BlockSpec((N_HEADS_A, BIAS_SPAN), lambda bi, i: (0, 0)),
                  pl.BlockSpec(new_blk, lambda bi, i: (bi, 0, 0)),
                  pl.BlockSpec(old_blk, lambda bi, i: (bi, 0, 0)),
                  pl.BlockSpec(new_blk, lambda bi, i: (bi, 0, 1)),
                  pl.BlockSpec(old_blk, lambda bi, i: (bi, 0, 0)),
                  pl.BlockSpec(new_blk, lambda bi, i: (bi, 0, 2))],
        out_specs=pl.BlockSpec(new_blk, lambda bi, i: (bi, 0, 0)),
        scratch_shapes=[pltpu.VMEM((N_HEADS_A, t, w + t), F32)],
        compiler_params=_params(2),
        name="band_attn_sample",
    )(bias_rows, z, cache_k, z, cache_v, z)


def _outproj_body(att_ref, u_ref, hist_ref, x_ref, wp_ref, ps_ref, w_ref, o_ref, cat_ref, *, pos0, tm,
                  zero_first_hist):
    i = pl.program_id(1)

    @pl.when(pl.program_id(2) == 0)
    def _():
        cat_ref[:, :A_WIDTH] = att_ref[...]
        hist = hist_ref[...].astype(F32)
        if zero_first_hist:
            hist = jnp.where(i > 0, hist, 0.0)
        ext = jnp.concatenate([hist, u_ref[...].astype(F32)], axis=0)
        pos = pos0 + i * tm + lax.broadcasted_iota(I32, (tm, 1), 0)
        for g, win in enumerate(POOL_WINDOWS):
            sl = slice(g * POOL_GROUP, (g + 1) * POOL_GROUP)
            e = ext[:, sl]
            run, span = e, 1
            while span < win:
                run = run[span:] + run[:-span]
                span *= 2
            first = HIST_ROWS - win + 1
            wsum = run[first:first + tm]
            cnt = jnp.minimum(pos + 1, win).astype(F32)
            d = (wsum / cnt - e[HIST_ROWS:]).astype(BF16)
            pool = jnp.dot(d, wp_ref[g], preferred_element_type=F32) * ps_ref[:, sl]
            cat_ref[:, A_WIDTH + g * POOL_GROUP:A_WIDTH + (g + 1) * POOL_GROUP] = pool.astype(BF16)

    o_ref[...] = x_ref[...] + jnp.dot(cat_ref[...], w_ref[...], preferred_element_type=F32)


def outproj(att, u_src, u_colblk, hist_src, hist_map, x, pool_w, pool_scale, w_out, pos0, tm, tn,
            zero_first_hist, name):
    b, s, d = x.shape
    tm = min(tm, s)
    return pl.pallas_call(
        functools.partial(_outproj_body, pos0=pos0, tm=tm, zero_first_hist=zero_first_hist),
        out_shape=jax.ShapeDtypeStruct((b, s, d), F32),
        grid=(b, s // tm, d // tn),
        in_specs=[pl.BlockSpec((None, tm, A_WIDTH), lambda bi, i, j: (bi, i, 0)),
                  pl.BlockSpec((None, tm, B_WIDTH), lambda bi, i, j: (bi, i, u_colblk)),
                  pl.BlockSpec((None, HIST_ROWS, B_WIDTH), hist_map),
                  pl.BlockSpec((None, tm, tn), lambda bi, i, j: (bi, i, j)),
                  pl.BlockSpec((len(POOL_WINDOWS), POOL_GROUP, POOL_GROUP), lambda bi, i, j: (0, 0, 0)),
                  pl.BlockSpec((1, B_WIDTH), lambda bi, i, j: (0, 0)),
                  pl.BlockSpec((A_WIDTH + B_WIDTH, tn), lambda bi, i, j: (0, j))],
        out_specs=pl.BlockSpec((None, tm, tn), lambda bi, i, j: (bi, i, j)),
        scratch_shapes=[pltpu.VMEM((tm, A_WIDTH + B_WIDTH), BF16)],
        compiler_params=_params(3),
        name=name,
    )(att, u_src, hist_src, x, pool_w, pool_scale.reshape(1, B_WIDTH), w_out)


def _memattn_body(q_ref, mk_ref, mv_ref, x_ref, w_ref, o_ref, oh_ref):
    @pl.when(pl.program_id(2) == 0)
    def _():
        scale = HEAD_DIM_MEM ** -0.5
        for h in range(N_HEADS_MEM):
            sl = slice(h * HEAD_DIM_MEM, (h + 1) * HEAD_DIM_MEM)
            s = lax.dot_general(q_ref[:, sl], mk_ref[:, sl].astype(BF16), (((1,), (1,)), ((), ())),
                                preferred_element_type=F32) * scale
            p = jnp.exp(s - jnp.max(s, axis=-1, keepdims=True))
            l = jnp.sum(p, axis=-1, keepdims=True)
            o = jnp.dot(p.astype(BF16), mv_ref[:, sl].astype(BF16), preferred_element_type=F32) / l
            oh_ref[:, sl] = o.astype(BF16)

    o_ref[...] = x_ref[...] + jnp.dot(oh_ref[...], w_ref[...], preferred_element_type=F32)


def mem_attention(q, mk, mv, x, w_mo, tm, tn, name):
    b, s, d = x.shape
    tm = min(tm, s)
    return pl.pallas_call(
        _memattn_body,
        out_shape=jax.ShapeDtypeStruct((b, s, d), F32),
        grid=(b, s // tm, d // tn),
        in_specs=[pl.BlockSpec((None, tm, d), lambda bi, i, j: (bi, i, 0)),
                  pl.BlockSpec((None, N_MEM, d), lambda bi, i, j: (bi, 0, 0)),
                  pl.BlockSpec((None, N_MEM, d), lambda bi, i, j: (bi, 0, 0)),
                  pl.BlockSpec((None, tm, tn), lambda bi, i, j: (bi, i, j)),
                  pl.BlockSpec((d, tn), lambda bi, i, j: (0, j))],
        out_specs=pl.BlockSpec((None, tm, tn), lambda bi, i, j: (bi, i, j)),
        scratch_shapes=[pltpu.VMEM((tm, d), BF16)],
        compiler_params=_params(3),
        name=name,
    )(q, mk, mv, x, w_mo)


def _router_body(x_ref, g_ref, w_ref, b_ref, xn_ref, ids_ref, gates_ref, rank_ref, cnt_ref, carry_ref, *, tm):
    @pl.when(pl.program_id(0) == 0)
    def _():
        carry_ref[...] = jnp.zeros_like(carry_ref)

    xn = _rms_scale(x_ref[...], g_ref[...])
    xn_ref[...] = xn
    x_hi = xn.astype(BF16)
    x_lo = (xn - x_hi.astype(F32)).astype(BF16)
    w = w_ref[...]
    w_hi = w.astype(BF16)
    w_lo = (w - w_hi.astype(F32)).astype(BF16)
    logits = (jnp.dot(x_hi, w_hi, preferred_element_type=F32) + jnp.dot(x_lo, w_hi, preferred_element_type=F32)
              + jnp.dot(x_hi, w_lo, preferred_element_type=F32)) + b_ref[...]

    lane = lax.broadcasted_iota(I32, (tm, N_EXPERTS), 1)
    kcol = lax.broadcasted_iota(I32, (tm, TOP_K), 1)
    work = logits
    multi_hot = jnp.zeros((tm, N_EXPERTS), F32)
    ids = jnp.zeros((tm, TOP_K), I32)
    tops = jnp.zeros((tm, TOP_K), F32)
    picks = []
    for k in range(TOP_K):
        best = jnp.max(work, axis=-1, keepdims=True)
        idx = jnp.min(jnp.where(work == best, lane, N_EXPERTS), axis=-1, keepdims=True)
        hit = lane == idx
        picks.append(hit)
        multi_hot = multi_hot + hit.astype(F32)
        ids = jnp.where(kcol == k, idx, ids)
        tops = jnp.where(kcol == k, best, tops)
        work = jnp.where(hit, -jnp.inf, work)
    e = jnp.exp(tops - tops[:, 0:1])
    gates_ref[...] = e / jnp.sum(e, axis=-1, keepdims=True)
    ids_ref[...] = ids

    r = lax.broadcasted_iota(I32, (tm, tm), 0)
    c = lax.broadcasted_iota(I32, (tm, tm), 1)
    earlier = (c < r).astype(BF16)
    before = jnp.dot(earlier, multi_hot.astype(BF16), preferred_element_type=F32) + carry_ref[...]
    rank = jnp.zeros((tm, TOP_K), F32)
    for k in range(TOP_K):
        rk = jnp.sum(jnp.where(picks[k], before, 0.0), axis=-1, keepdims=True)
        rank = jnp.where(kcol == k, rk, rank)
    rank_ref[...] = rank.astype(I32)
    total = carry_ref[...] + jnp.sum(multi_hot, axis=0, keepdims=True)
    carry_ref[...] = total
    cnt_ref[...] = total.astype(I32)


def router(x, g, w_router, b_router, tm, name):
    n, d = x.shape
    tm = min(tm, n)
    row = lambda i: (i, 0)
    fixed = lambda i: (0, 0)
    return pl.pallas_call(
        functools.partial(_router_body, tm=tm),
        out_shape=(jax.ShapeDtypeStruct((n, d), F32),
                   jax.ShapeDtypeStruct((n, TOP_K), I32),
                   jax.ShapeDtypeStruct((n, TOP_K), F32),
                   jax.ShapeDtypeStruct((n, TOP_K), I32),
                   jax.ShapeDtypeStruct((1, N_EXPERTS), I32)),
        grid=(n // tm,),
        in_specs=[pl.BlockSpec((tm, d), row), pl.BlockSpec((1, d), fixed),
                  pl.BlockSpec((d, N_EXPERTS), fixed), pl.BlockSpec((1, N_EXPERTS), fixed)],
        out_specs=(pl.BlockSpec((tm, d), row), pl.BlockSpec((tm, TOP_K), row), pl.BlockSpec((tm, TOP_K), row),
                   pl.BlockSpec((tm, TOP_K), row), pl.BlockSpec((1, N_EXPERTS), fixed)),
        scratch_shapes=[pltpu.VMEM((1, N_EXPERTS), F32)],
        compiler_params=_params(1),
        name=name,
    )(x, g.reshape(1, d), w_router, b_router.reshape(1, N_EXPERTS))


def _dest_body(pstart_ref, ids_ref, rank_ref, dest_ref):
    ids = ids_ref[...]
    dest = rank_ref[...]
    for e in range(N_EXPERTS):
        dest = dest + jnp.where(ids == e, pstart_ref[e], 0)
    dest_ref[...] = dest


def assignment_dest(pstart, ids, rank, name):
    n_assign = ids.size
    shape = (n_assign // LANES, LANES)
    spec = pl.BlockSpec(shape, lambda i, ps: (0, 0))
    dest = pl.pallas_call(
        _dest_body,
        out_shape=jax.ShapeDtypeStruct(shape, I32),
        grid_spec=pltpu.PrefetchScalarGridSpec(num_scalar_prefetch=1, grid=(1,), in_specs=[spec, spec],
                                               out_specs=spec),
        compiler_params=_params(1),
        name=name,
    )(pstart, ids.reshape(shape), rank.reshape(shape))
    return dest.reshape(n_assign)


def _dispatch_body(fill_ref, cnt_ref, dest_ref, x_ref, xbuf_hbm, zero_ref, sem, zsem, *, tc):
    def issue(t, carry):
        for k in range(TOP_K):
            pltpu.make_async_copy(x_ref.at[pl.ds(t, 1)], xbuf_hbm.at[pl.ds(dest_ref[0, 0, t * TOP_K + k], 1)],
                                  sem).start()
        return carry

    lax.fori_loop(0, tc, issue, 0, unroll=4)

    @pl.when(pl.program_id(0) == 0)
    def _():
        zero_ref[...] = jnp.zeros_like(zero_ref)

        def pad_copy(dst):
            return pltpu.make_async_copy(zero_ref.at[pl.ds(0, 1)], xbuf_hbm.at[pl.ds(dst, 1)], zsem)

        def tail_copy(c):
            return pltpu.make_async_copy(zero_ref, xbuf_hbm.at[pl.ds(pl.multiple_of(c * ZERO_ROWS, ZERO_ROWS),
                                                                     ZERO_ROWS)], zsem)

        first_tail = (fill_ref[N_EXPERTS - 1] + cnt_ref[N_EXPERTS - 1]) // ZERO_ROWS
        n_tail = xbuf_hbm.shape[0] // ZERO_ROWS

        def tail_start(c, carry):
            tail_copy(c).start()
            return carry

        def tail_wait(c, carry):
            tail_copy(c).wait()
            return carry

        lax.fori_loop(first_tail, n_tail, tail_start, 0)
        lax.fori_loop(first_tail, n_tail, tail_wait, 0)

        def per_expert(e, carry):
            start = fill_ref[e]
            n_pad = cnt_ref[e]

            def one(r, c2):
                pad_copy(start + r).start()
                return c2

            lax.fori_loop(0, n_pad, one, 0)

            def one_wait(r, c2):
                pad_copy(start + r).wait()
                return c2

            lax.fori_loop(0, n_pad, one_wait, 0)
            return carry

        lax.fori_loop(0, N_EXPERTS, per_expert, 0)

    for k in range(TOP_K):
        pltpu.make_async_copy(x_ref, xbuf_hbm.at[pl.ds(0, tc)], sem).wait()


def dispatch(xn, dest, pad_start, pad_count, n_rows, tc, name):
    n, d = xn.shape
    tc = min(tc, n)
    return pl.pallas_call(
        functools.partial(_dispatch_body, tc=tc),
        out_shape=jax.ShapeDtypeStruct((n_rows, d), xn.dtype),
        grid_spec=pltpu.PrefetchScalarGridSpec(
            num_scalar_prefetch=2,
            grid=(n // tc,),
            in_specs=[pl.BlockSpec((1, 1, tc * TOP_K), lambda i, a, b: (i, 0, 0), memory_space=pltpu.SMEM),
                      pl.BlockSpec((tc, d), lambda i, a, b: (i, 0))],
            out_specs=pl.BlockSpec(memory_space=pl.ANY),
            scratch_shapes=[pltpu.VMEM((ZERO_ROWS, d), xn.dtype), pltpu.SemaphoreType.DMA,
                            pltpu.SemaphoreType.DMA]),
        compiler_params=_params(1),
        name=name,
    )(pad_start, pad_count, dest.reshape(n // tc, 1, tc * TOP_K), xn)


def _expert_body(be_ref, nu_ref, x_ref, wg_ref, wu_ref, bg_ref, bu_ref, wd_ref, bd_ref, o_ref, xb_ref, acc_ref):
    j = pl.program_id(1)

    @pl.when(pl.program_id(0) < nu_ref[0])
    def _():
        @pl.when(j == 0)
        def _():
            xb_ref[...] = x_ref[...].astype(BF16)
            acc_ref[...] = jnp.broadcast_to(bd_ref[...], acc_ref.shape)

        xb = xb_ref[...]
        gate = jnp.dot(xb, wg_ref[...], preferred_element_type=F32) + bg_ref[...]
        up = jnp.dot(xb, wu_ref[...], preferred_element_type=F32) + bu_ref[...]
        gate = jnp.minimum(gate, SWIGLU_LIMIT)
        up = jnp.clip(up, -SWIGLU_LIMIT, SWIGLU_LIMIT)
        act = (up + 1.0) * (gate * jax.nn.sigmoid(SWIGLU_ALPHA * gate))
        acc_ref[...] += jnp.dot(act.astype(BF16), wd_ref[...], preferred_element_type=F32)

        @pl.when(j == pl.num_programs(1) - 1)
        def _():
            o_ref[...] = acc_ref[...]

    @pl.when((pl.program_id(0) >= nu_ref[0]) & (j == 0))
    def _():
        o_ref[...] = jnp.zeros_like(o_ref)


def expert_ffn(block_expert, n_used, xbuf, w_gu, b_gu, w_down, b_down, bs, tf, name):
    n_rows, d = xbuf.shape
    n_blocks = n_rows // bs
    nt = D_FF // tf

    def blk(b, nu):
        return jnp.minimum(b, nu[0] - 1)

    def ff(b, j, nu):
        return jnp.where(b < nu[0], j, nt - 1)

    return pl.pallas_call(
        _expert_body,
        out_shape=jax.ShapeDtypeStruct((n_rows, d), F32),
        grid_spec=pltpu.PrefetchScalarGridSpec(
            num_scalar_prefetch=2,
            grid=(n_blocks, nt),
            in_specs=[pl.BlockSpec((bs, d), lambda b, j, be, nu: (blk(b, nu), 0)),
                      pl.BlockSpec((None, d, tf), lambda b, j, be, nu: (be[blk(b, nu)], 0, ff(b, j, nu))),
                      pl.BlockSpec((None, d, tf), lambda b, j, be, nu: (be[blk(b, nu)], 0, nt + ff(b, j, nu))),
                      pl.BlockSpec((None, 1, tf), lambda b, j, be, nu: (be[blk(b, nu)], 0, ff(b, j, nu))),
                      pl.BlockSpec((None, 1, tf), lambda b, j, be, nu: (be[blk(b, nu)], 0, nt + ff(b, j, nu))),
                      pl.BlockSpec((None, tf, d), lambda b, j, be, nu: (be[blk(b, nu)], ff(b, j, nu), 0)),
                      pl.BlockSpec((None, 1, d), lambda b, j, be, nu: (be[blk(b, nu)], 0, 0))],
            out_specs=pl.BlockSpec((bs, d), lambda b, j, be, nu: (b, 0)),
            scratch_shapes=[pltpu.VMEM((bs, d), BF16), pltpu.VMEM((bs, d), F32)]),
        compiler_params=_params(2),
        name=name,
    )(block_expert, n_used, xbuf, w_gu, w_gu, b_gu.reshape(N_EXPERTS, 1, 2 * D_FF),
      b_gu.reshape(N_EXPERTS, 1, 2 * D_FF), w_down, b_down.reshape(N_EXPERTS, 1, d))


def _combine_body(dest_ref, next_ref, gates_ref, h_ref, g_ref, ybuf_hbm, o_ref, buf_ref, sem, *, tc):
    i = pl.program_id(0)
    slot = i % 2

    def issue_tile(idx_ref, s):
        def issue(t, carry):
            for k in range(TOP_K):
                pltpu.make_async_copy(ybuf_hbm.at[pl.ds(idx_ref[0, 0, t * TOP_K + k], 1)],
                                      buf_ref.at[s, k, pl.ds(t, 1)], sem.at[s]).start()
            return carry

        lax.fori_loop(0, tc, issue, 0, unroll=4)

    @pl.when(i == 0)
    def _():
        issue_tile(dest_ref, slot)

    @pl.when(i + 1 < pl.num_programs(0))
    def _():
        issue_tile(next_ref, 1 - slot)

    for k in range(TOP_K):
        pltpu.make_async_copy(ybuf_hbm.at[pl.ds(0, tc)], buf_ref.at[slot, k], sem.at[slot]).wait()

    gates = gates_ref[...]
    y = h_ref[...]
    for k in range(TOP_K):
        y = y + buf_ref[slot, k] * gates[:, k:k + 1]
    o_ref[...] = _rms_scale(y, g_ref[...])


def combine(dest, gates, h, g_final, ybuf, tc, name):
    n, d = h.shape
    tc = min(tc, n)
    n_tiles = n // tc
    dest3 = dest.reshape(n_tiles, 1, tc * TOP_K)
    return pl.pallas_call(
        functools.partial(_combine_body, tc=tc),
        out_shape=jax.ShapeDtypeStruct((n, d), F32),
        grid=(n_tiles,),
        in_specs=[pl.BlockSpec((1, 1, tc * TOP_K), lambda i: (i, 0, 0), memory_space=pltpu.SMEM),
                  pl.BlockSpec((1, 1, tc * TOP_K), lambda i: (jnp.minimum(i + 1, n_tiles - 1), 0, 0),
                               memory_space=pltpu.SMEM),
                  pl.BlockSpec((tc, TOP_K), lambda i: (i, 0)),
                  pl.BlockSpec((tc, d), lambda i: (i, 0)),
                  pl.BlockSpec((1, d), lambda i: (0, 0)),
                  pl.BlockSpec(memory_space=pl.ANY)],
        out_specs=pl.BlockSpec((tc, d), lambda i: (i, 0)),
        scratch_shapes=[pltpu.VMEM((2, TOP_K, tc, d), F32), pltpu.SemaphoreType.DMA((2,))],
        compiler_params=_params(1),
        name=name,
    )(dest3, dest3, gates, h, g_final.reshape(1, d), ybuf)


def moe_and_final_norm(h, g_ffn, w_router, b_router, w_gu, b_gu, w_down, b_down, g_final, *, router_tm, bs, tf,
                       dispatch_tc, combine_tc, tag):
    n, d = h.shape
    n_assign = n * TOP_K
    n_blocks = n_assign // bs + N_EXPERTS
    xn, ids, gates, rank, counts = router(h, g_ffn, w_router, b_router, router_tm, "router_" + tag)
    counts = counts.reshape(N_EXPERTS)
    padded = (counts + bs - 1) // bs * bs
    pend = jnp.cumsum(padded)
    pstart = pend - padded
    n_used = (pend[-1:] // bs).astype(I32)
    block_row0 = jnp.arange(n_blocks, dtype=I32) * bs
    block_expert = jnp.minimum(jnp.sum((pend[None, :] <= block_row0[:, None]).astype(I32), axis=1), N_EXPERTS - 1)
    dest = assignment_dest(pstart.astype(I32), ids, rank, "dest_" + tag)
    xbuf = dispatch(xn, dest, (pstart + counts).astype(I32), (padded - counts).astype(I32), n_blocks * bs,
                    dispatch_tc, "dispatch_" + tag)
    ybuf = expert_ffn(block_expert, n_used, xbuf, w_gu, b_gu, w_down, b_down, bs, tf, "experts_" + tag)
    return combine(dest, gates, h, g_final, ybuf, combine_tc, "combine_" + tag)


def _bias_rows(table):
    flipped = table[:, ::-1]
    left = BIAS_SPAN - 3 * REL_MAX
    right = BIAS_SPAN - left - (2 * REL_MAX + 1)
    return jnp.pad(flipped, ((0, 0), (left, right)), mode="edge").astype(F32)


def kernel(x_prompt, x_sample, cache_attn_k, cache_attn_v, state_pool, cache_mem_k, cache_mem_v, mem_prompt, g_mix,
           w_in, rel_table, pool_w, pool_scale, w_out, g_memkv, w_mk, w_mv, g_mem, w_mq, w_mo, g_ffn, w_router,
           b_router, w_gu, b_gu, w_down, b_down, g_final):
    bp, s, d = x_prompt.shape
    bsz, t, _ = x_sample.shape
    l = 0
    keep = min(ATT_WINDOW, s)

    w_in_b = w_in[l].astype(BF16)
    w_out_b = w_out[l].astype(BF16)
    w_mk_b, w_mv_b = w_mk[l].astype(BF16), w_mv[l].astype(BF16)
    w_mq_b, w_mo_b = w_mq[l].astype(BF16), w_mo[l].astype(BF16)
    pool_w_b = pool_w[l].astype(BF16)
    w_gu_b, w_down_b = w_gu[l].astype(BF16), w_down[l].astype(BF16)
    bias_rows = _bias_rows(rel_table[l])
    experts = (g_ffn[l], w_router[l], b_router[l], w_gu_b, b_gu[l], w_down_b, b_down[l], g_final)

    z = norm_matmul(x_prompt.reshape(bp * s, d), g_mix[l], w_in_b, BF16, 1024, 1024, "in_proj_p")
    z = z.reshape(bp, s, IN_WIDTH)
    att = band_attention_prompt(z, bias_rows, 256)
    tm_o = 512
    h = outproj(att, z, 3, z, lambda bi, i, j: (bi, jnp.maximum(i * (tm_o // HIST_ROWS) - 1, 0), 3),
                x_prompt, pool_w_b, pool_scale[l], w_out_b, 0, tm_o, 1024, True, "outproj_p")
    mem2d = mem_prompt.reshape(bp * N_MEM, d)
    mk = norm_matmul(mem2d, g_memkv[l], w_mk_b, F32, 1024, 1024, "mem_k_p").reshape(bp, N_MEM, d)
    mv = norm_matmul(mem2d, g_memkv[l], w_mv_b, F32, 1024, 1024, "mem_v_p").reshape(bp, N_MEM, d)
    qm = norm_matmul(h.reshape(bp * s, d), g_mem[l], w_mq_b, BF16, 1024, 1024, "mem_q_p").reshape(bp, s, d)
    h = mem_attention(qm, mk, mv, h, w_mo_b, 512, 1024, "mem_attn_p")
    y_prompt = moe_and_final_norm(h.reshape(bp * s, d), *experts, router_tm=512, bs=512, tf=512,
                                  dispatch_tc=512, combine_tc=256, tag="p").reshape(bp, s, d)

    new_k_p = z[:, s - keep:, A_WIDTH:2 * A_WIDTH].astype(F32).reshape(1, bp, keep, N_HEADS_A, HEAD_DIM_A)
    new_v_p = z[:, s - keep:, 2 * A_WIDTH:3 * A_WIDTH].astype(F32).reshape(1, bp, keep, N_HEADS_A, HEAD_DIM_A)
    new_pool_p = z[:, s - POOL_HIST:, 3 * A_WIDTH:].astype(F32)[None]
    new_mk_p = mk.reshape(1, bp, N_MEM, N_HEADS_MEM, HEAD_DIM_MEM)
    new_mv_p = mv.reshape(1, bp, N_MEM, N_HEADS_MEM, HEAD_DIM_MEM)

    zs = norm_matmul(x_sample.reshape(bsz * t, d), g_mix[l], w_in_b, BF16, 256, 1024, "in_proj_s")
    zs = zs.reshape(bsz, t, IN_WIDTH)
    w_cache = cache_attn_k.shape[2]
    att_s = band_attention_sample(zs, cache_attn_k[l].reshape(bsz, w_cache, A_WIDTH),
                                  cache_attn_v[l].reshape(bsz, w_cache, A_WIDTH), bias_rows)
    hist_s = jnp.pad(state_pool[l], ((0, 0), (HIST_ROWS - POOL_HIST, 0), (0, 0)))
    hs = outproj(att_s, zs, 3, hist_s, lambda bi, i, j: (bi, 0, 0), x_sample, pool_w_b, pool_scale[l], w_out_b,
                 PAST_LEN, t, 1024, False, "outproj_s")
    qs = norm_matmul(hs.reshape(bsz * t, d), g_mem[l], w_mq_b, BF16, 256, 1024, "mem_q_s").reshape(bsz, t, d)
    hs = mem_attention(qs, cache_mem_k[l].reshape(bsz, N_MEM, d), cache_mem_v[l].reshape(bsz, N_MEM, d), hs,
                       w_mo_b, t, 1024, "mem_attn_s")
    y_sample = moe_and_final_norm(hs.reshape(bsz * t, d), *experts, router_tm=256, bs=128, tf=512,
                                  dispatch_tc=256, combine_tc=128, tag="s").reshape(bsz, t, d)

    new_k_s = zs[:, :, A_WIDTH:2 * A_WIDTH].astype(F32).reshape(1, bsz, t, N_HEADS_A, HEAD_DIM_A)
    new_v_s = zs[:, :, 2 * A_WIDTH:3 * A_WIDTH].astype(F32).reshape(1, bsz, t, N_HEADS_A, HEAD_DIM_A)
    us_ext = jnp.concatenate([state_pool[l], zs[:, :, 3 * A_WIDTH:].astype(F32)], axis=1)
    new_pool_s = us_ext[:, -POOL_HIST:][None]

    return (y_prompt, y_sample, new_k_p, new_v_p, new_pool_p, new_mk_p, new_mv_p, new_k_s, new_v_s, new_pool_s)
```

```python
import functools
from typing import NamedTuple, Optional

import jax
import jax.numpy as jnp
from jax import lax
from jax.experimental import pallas as pl
from jax.experimental.pallas import tpu as pltpu

F32 = jnp.float32
BF16 = jnp.bfloat16
I32 = jnp.int32

D_MODEL = 2048
PAST_LEN = 4096
CHUNK = 64
N_LEFT_CHUNKS = 8
ATT_WINDOW = N_LEFT_CHUNKS * CHUNK
A_WIDTH = 1024
N_HEADS_A = 8
HEAD_DIM_A = A_WIDTH // N_HEADS_A
REL_MAX = 128
B_WIDTH = 1024
POOL_WINDOWS = (2, 4, 8, 16)
POOL_GROUP = B_WIDTH // len(POOL_WINDOWS)
POOL_HIST = max(POOL_WINDOWS) - 1
HIST_ROWS = POOL_HIST + 1
IN_WIDTH = 3 * A_WIDTH + B_WIDTH
N_MEM = 256
N_HEADS_MEM = 4
HEAD_DIM_MEM = D_MODEL // N_HEADS_MEM
N_EXPERTS = 32
TOP_K = 4
D_FF = D_MODEL
SWIGLU_LIMIT = 7.0
SWIGLU_ALPHA = 1.702
EPS = 1e-5
NEG_INF = -1e30

V7X_VMEM_LIMIT_BYTES = 56 * 1024 * 1024
LANES = 128
BIAS_SPAN = 1024
ZERO_ROWS = 128


def _params(n_grid):
    return pltpu.CompilerParams(dimension_semantics=("arbitrary",) * n_grid,
                                vmem_limit_bytes=V7X_VMEM_LIMIT_BYTES)


def _rms_scale(x, g):
    return x * lax.rsqrt(jnp.mean(x * x, axis=-1, keepdims=True) + EPS) * g


class CastJob(NamedTuple):
    src: jax.Array
    col_block: int
    cols: int


def _call_with_cast_job(body, job, grid, in_specs, out_specs, out_shape, scratch_shapes, args, name):
    n_grid = len(grid)
    if job is None:
        return pl.pallas_call(body, out_shape=out_shape, grid=grid, in_specs=in_specs, out_specs=out_specs,
                              scratch_shapes=scratch_shapes, compiler_params=_params(n_grid), name=name)(*args)
    n_steps = 1
    for g in grid:
        n_steps *= g
    rows = job.src.shape[0]
    step_rows = rows // n_steps
    assert step_rows * n_steps == rows and step_rows % 16 == 0

    def flat_step(*idx):
        flat = idx[0]
        for g, i in zip(grid[1:], idx[1:]):
            flat = flat * g + i
        return flat

    n_in, n_out = len(in_specs), len(out_specs)

    def body_with_cast(*refs):
        ins, src_ref, rest = refs[:n_in], refs[n_in], refs[n_in + 1:]
        outs, dst_ref, scratch = rest[:n_out], rest[n_out], rest[n_out + 1:]
        body(*ins, *outs, *scratch)
        dst_ref[...] = src_ref[...].astype(BF16)

    return pl.pallas_call(
        body_with_cast,
        out_shape=tuple(out_shape) + (jax.ShapeDtypeStruct((rows, job.cols), BF16),),
        grid=grid,
        in_specs=list(in_specs) + [pl.BlockSpec((step_rows, job.cols), lambda *idx: (flat_step(*idx), job.col_block))],
        out_specs=tuple(out_specs) + (pl.BlockSpec((step_rows, job.cols), lambda *idx: (flat_step(*idx), 0)),),
        scratch_shapes=scratch_shapes,
        compiler_params=_params(n_grid),
        name=name,
    )(*args, job.src)


def _norm_matmul_body(x_ref, g_ref, w_ref, o_ref, xn_ref):
    @pl.when(pl.program_id(1) == 0)
    def _():
        xn_ref[...] = _rms_scale(x_ref[...].astype(F32), g_ref[...]).astype(BF16)

    o_ref[...] = jnp.dot(xn_ref[...], w_ref[...], preferred_element_type=F32).astype(o_ref.dtype)


def norm_matmul(x, g, w, out_dtype, tm, tn, name, cast_job=None):
    m, d = x.shape
    n = w.shape[1]
    tm, tn = min(tm, m), min(tn, n)
    out = _call_with_cast_job(
        _norm_matmul_body, cast_job, (m // tm, n // tn),
        [pl.BlockSpec((tm, d), lambda i, j: (i, 0)),
         pl.BlockSpec((1, d), lambda i, j: (0, 0)),
         pl.BlockSpec((d, tn), lambda i, j: (0, j))],
        (pl.BlockSpec((tm, tn), lambda i, j: (i, j)),),
        (jax.ShapeDtypeStruct((m, n), out_dtype),),
        [pltpu.VMEM((tm, d), BF16)],
        (x, g.reshape(1, d), w), name)
    return out[0] if cast_job is None else out


def _build_band_bias(g_ref, bias_ref, tq, tk):
    row = lax.broadcasted_iota(I32, (tq, tk), 0)
    col = lax.broadcasted_iota(I32, (tq, tk), 1)
    rel_chunk = col // CHUNK - N_LEFT_CHUNKS - row // CHUNK
    in_band = (rel_chunk <= 0) & (rel_chunk >= -N_LEFT_CHUNKS)
    lo = BIAS_SPAN - ATT_WINDOW - 2 * REL_MAX
    for h in range(N_HEADS_A):
        base = jnp.broadcast_to(g_ref[h:h + 1, :], (tq, BIAS_SPAN))
        rolled = pltpu.roll(base, 0, 1, stride=1, stride_axis=0)
        bias_ref[h] = jnp.where(in_band, rolled[:, lo:lo + tk], NEG_INF)


def _band_attn_body(*refs, n_kv, tq, kw, dynamic_first):
    g_ref, q_ref = refs[0], refs[1]
    k_refs = refs[2:2 + n_kv]
    v_refs = refs[2 + n_kv:2 + 2 * n_kv]
    o_ref, bias_ref = refs[2 + 2 * n_kv], refs[3 + 2 * n_kv]
    tk = sum(kw)
    first = (pl.program_id(0) == 0) & (pl.program_id(1) == 0)

    @pl.when(first)
    def _():
        _build_band_bias(g_ref, bias_ref, tq, tk)

    scale = HEAD_DIM_A ** -0.5
    if dynamic_first:
        col = lax.broadcasted_iota(I32, (tq, tk), 1)
        exists = col >= ATT_WINDOW - pl.program_id(1) * tq
    for h in range(N_HEADS_A):
        sl = slice(h * HEAD_DIM_A, (h + 1) * HEAD_DIM_A)
        q = q_ref[:, sl]
        s = jnp.concatenate(
            [lax.dot_general(q, k_ref[:, sl].astype(BF16), (((1,), (1,)), ((), ())), preferred_element_type=F32)
             for k_ref in k_refs], axis=1)
        s = s * scale + bias_ref[h]
        if dynamic_first:
            s = jnp.where(exists, s, NEG_INF)
        m = jnp.max(s, axis=-1, keepdims=True)
        p = jnp.exp(s - m)
        l = jnp.sum(p, axis=-1, keepdims=True)
        pb = p.astype(BF16)
        o = None
        off = 0
        for v_ref, w in zip(v_refs, kw):
            part = jnp.dot(pb[:, off:off + w], v_ref[:, sl].astype(BF16), preferred_element_type=F32)
            o = part if o is None else o + part
            off += w
        o_ref[:, sl] = (o / l).astype(o_ref.dtype)


def band_attention_prompt(z, bias_rows, tq):
    b, s, _ = z.shape
    n_kv = ATT_WINDOW // tq + 1
    tk = n_kv * tq
    blk = (None, tq, A_WIDTH)

    def kv_spec(back, colblk):
        return pl.BlockSpec(blk, lambda bi, i: (bi, jnp.maximum(i - back, 0), colblk))

    in_specs = [pl.BlockSpec((N_HEADS_A, BIAS_SPAN), lambda bi, i: (0, 0)),
                pl.BlockSpec(blk, lambda bi, i: (bi, i, 0))]
    in_specs += [kv_spec(n_kv - 1 - j, 1) for j in range(n_kv)]
    in_specs += [kv_spec(n_kv - 1 - j, 2) for j in range(n_kv)]
    return pl.pallas_call(
        functools.partial(_band_attn_body, n_kv=n_kv, tq=tq, kw=(tq,) * n_kv, dynamic_first=True),
        out_shape=jax.ShapeDtypeStruct((b, s, A_WIDTH), BF16),
        grid=(b, s // tq),
        in_specs=in_specs,
        out_specs=pl.BlockSpec(blk, lambda bi, i: (bi, i, 0)),
        scratch_shapes=[pltpu.VMEM((N_HEADS_A, tq, tk), F32)],
        compiler_params=_params(2),
        name="band_attn_prompt",
    )(bias_rows, *([z] * (1 + 2 * n_kv)))


def band_attention_sample(z, cache_k, cache_v, bias_rows):
    b, t, _ = z.shape
    w = cache_k.shape[1]
    assert w == ATT_WINDOW and PAST_LEN >= w and PAST_LEN % CHUNK == 0 and t <= CHUNK
    new_blk = (None, t, A_WIDTH)
    old_blk = (None, w, A_WIDTH)
    return pl.pallas_call(
        functools.partial(_band_attn_body, n_kv=2, tq=t, kw=(w, t), dynamic_first=False),
        out_shape=jax.ShapeDtypeStruct((b, t, A_WIDTH), BF16),
        grid=(b, 1),
        in_specs=[pl.BlockSpec((N_HEADS_A, BIAS_SPAN), lambda bi, i: (0, 0)),
                  pl.BlockSpec(new_blk, lambda bi, i: (bi, 0, 0)),
                  pl.BlockSpec(old_blk, lambda bi, i: (bi, 0, 0)),
                  pl.BlockSpec(new_blk, lambda bi, i: (bi, 0, 1)),
                  pl.BlockSpec(old_blk, lambda bi, i: (bi, 0, 0)),
                  pl.BlockSpec(new_blk, lambda bi, i: (bi, 0, 2))],
        out_specs=pl.BlockSpec(new_blk, lambda bi, i: (bi, 0, 0)),
        scratch_shapes=[pltpu.VMEM((N_HEADS_A, t, w + t), F32)],
        compiler_params=_params(2),
        name="band_attn_sample",
    )(bias_rows, z, cache_k, z, cache_v, z)


def _outproj_body(att_ref, u_ref, hist_ref, x_ref, wp_ref, ps_ref, w_ref, o_ref, cat_ref, *, pos0, tm,
                  zero_first_hist):
    i = pl.program_id(1)

    @pl.when(pl.program_id(2) == 0)
    def _():
        cat_ref[:, :A_WIDTH] = att_ref[...]
        hist = hist_ref[...].astype(F32)
        if zero_first_hist:
            hist = jnp.where(i > 0, hist, 0.0)
        ext = jnp.concatenate([hist, u_ref[...].astype(F32)], axis=0)
        pos = pos0 + i * tm + lax.broadcasted_iota(I32, (tm, 1), 0)
        for g, win in enumerate(POOL_WINDOWS):
            sl = slice(g * POOL_GROUP, (g + 1) * POOL_GROUP)
            e = ext[:, sl]
            run, span = e, 1
            while span < win:
                run = run[span:] + run[:-span]
                span *= 2
            first = HIST_ROWS - win + 1
            wsum = run[first:first + tm]
            cnt = jnp.minimum(pos + 1, win).astype(F32)
            d = (wsum / cnt - e[HIST_ROWS:]).astype(BF16)
            pool = jnp.dot(d, wp_ref[g], preferred_element_type=F32) * ps_ref[:, sl]
            cat_ref[:, A_WIDTH + g * POOL_GROUP:A_WIDTH + (g + 1) * POOL_GROUP] = pool.astype(BF16)

    o_ref[...] = x_ref[...] + jnp.dot(cat_ref[...], w_ref[...], preferred_element_type=F32)


def outproj(att, u_src, u_colblk, hist_src, hist_map, x, pool_w, pool_scale, w_out, pos0, tm, tn,
            zero_first_hist, name, cast_job=None):
    b, s, d = x.shape
    tm = min(tm, s)
    out = _call_with_cast_job(
        functools.partial(_outproj_body, pos0=pos0, tm=tm, zero_first_hist=zero_first_hist), cast_job,
        (b, s // tm, d // tn),
        [pl.BlockSpec((None, tm, A_WIDTH), lambda bi, i, j: (bi, i, 0)),
         pl.BlockSpec((None, tm, B_WIDTH), lambda bi, i, j: (bi, i, u_colblk)),
         pl.BlockSpec((None, HIST_ROWS, B_WIDTH), hist_map),
         pl.BlockSpec((None, tm, tn), lambda bi, i, j: (bi, i, j)),
         pl.BlockSpec((len(POOL_WINDOWS), POOL_GROUP, POOL_GROUP), lambda bi, i, j: (0, 0, 0)),
         pl.BlockSpec((1, B_WIDTH), lambda bi, i, j: (0, 0)),
         pl.BlockSpec((A_WIDTH + B_WIDTH, tn), lambda bi, i, j: (0, j))],
        (pl.BlockSpec((None, tm, tn), lambda bi, i, j: (bi, i, j)),),
        (jax.ShapeDtypeStruct((b, s, d), F32),),
        [pltpu.VMEM((tm, A_WIDTH + B_WIDTH), BF16)],
        (att, u_src, hist_src, x, pool_w, pool_scale.reshape(1, B_WIDTH), w_out), name)
    return out[0] if cast_job is None else out


def _memattn_body(q_ref, mk_ref, mv_ref, x_ref, w_ref, o_ref, oh_ref):
    @pl.when(pl.program_id(2) == 0)
    def _():
        scale = HEAD_DIM_MEM ** -0.5
        for h in range(N_HEADS_MEM):
            sl = slice(h * HEAD_DIM_MEM, (h + 1) * HEAD_DIM_MEM)
            s = lax.dot_general(q_ref[:, sl], mk_ref[:, sl].astype(BF16), (((1,), (1,)), ((), ())),
                                preferred_element_type=F32) * scale
            p = jnp.exp(s - jnp.max(s, axis=-1, keepdims=True))
            l = jnp.sum(p, axis=-1, keepdims=True)
            o = jnp.dot(p.astype(BF16), mv_ref[:, sl].astype(BF16), preferred_element_type=F32) / l
            oh_ref[:, sl] = o.astype(BF16)

    o_ref[...] = x_ref[...] + jnp.dot(oh_ref[...], w_ref[...], preferred_element_type=F32)


def mem_attention(q, mk, mv, x, w_mo, tm, tn, name, cast_job=None):
    b, s, d = x.shape
    tm = min(tm, s)
    out = _call_with_cast_job(
        _memattn_body, cast_job, (b, s // tm, d // tn),
        [pl.BlockSpec((None, tm, d), lambda bi, i, j: (bi, i, 0)),
         pl.BlockSpec((None, N_MEM, d), lambda bi, i, j: (bi, 0, 0)),
         pl.BlockSpec((None, N_MEM, d), lambda bi, i, j: (bi, 0, 0)),
         pl.BlockSpec((None, tm, tn), lambda bi, i, j: (bi, i, j)),
         pl.BlockSpec((d, tn), lambda bi, i, j: (0, j))],
        (pl.BlockSpec((None, tm, tn), lambda bi, i, j: (bi, i, j)),),
        (jax.ShapeDtypeStruct((b, s, d), F32),),
        [pltpu.VMEM((tm, d), BF16)],
        (q, mk, mv, x, w_mo), name)
    return out[0] if cast_job is None else out


def _router_body(x_ref, g_ref, w_ref, b_ref, cin_ref, xn_ref, ids_ref, gates_ref, rank_ref, cnt_ref, carry_ref, *,
                 tm):
    @pl.when(pl.program_id(0) == 0)
    def _():
        carry_ref[...] = cin_ref[...].astype(F32)

    xn = _rms_scale(x_ref[...], g_ref[...])
    xn_ref[...] = xn
    x_hi = xn.astype(BF16)
    x_lo = (xn - x_hi.astype(F32)).astype(BF16)
    w = w_ref[...]
    w_hi = w.astype(BF16)
    w_lo = (w - w_hi.astype(F32)).astype(BF16)
    logits = (jnp.dot(x_hi, w_hi, preferred_element_type=F32) + jnp.dot(x_lo, w_hi, preferred_element_type=F32)
              + jnp.dot(x_hi, w_lo, preferred_element_type=F32)) + b_ref[...]

    lane = lax.broadcasted_iota(I32, (tm, N_EXPERTS), 1)
    kcol = lax.broadcasted_iota(I32, (tm, TOP_K), 1)
    work = logits
    multi_hot = jnp.zeros((tm, N_EXPERTS), F32)
    ids = jnp.zeros((tm, TOP_K), I32)
    tops = jnp.zeros((tm, TOP_K), F32)
    picks = []
    for k in range(TOP_K):
        best = jnp.max(work, axis=-1, keepdims=True)
        idx = jnp.min(jnp.where(work == best, lane, N_EXPERTS), axis=-1, keepdims=True)
        hit = lane == idx
        picks.append(hit)
        multi_hot = multi_hot + hit.astype(F32)
        ids = jnp.where(kcol == k, idx, ids)
        tops = jnp.where(kcol == k, best, tops)
        work = jnp.where(hit, -jnp.inf, work)
    e = jnp.exp(tops - tops[:, 0:1])
    gates_ref[...] = e / jnp.sum(e, axis=-1, keepdims=True)
    ids_ref[...] = ids

    r = lax.broadcasted_iota(I32, (tm, tm), 0)
    c = lax.broadcasted_iota(I32, (tm, tm), 1)
    earlier = (c < r).astype(BF16)
    before = jnp.dot(earlier, multi_hot.astype(BF16), preferred_element_type=F32) + carry_ref[...]
    rank = jnp.zeros((tm, TOP_K), F32)
    for k in range(TOP_K):
        rk = jnp.sum(jnp.where(picks[k], before, 0.0), axis=-1, keepdims=True)
        rank = jnp.where(kcol == k, rk, rank)
    rank_ref[...] = rank.astype(I32)
    total = carry_ref[...] + jnp.sum(multi_hot, axis=0, keepdims=True)
    carry_ref[...] = total
    cnt_ref[...] = total.astype(I32)


def router(x, g, w_router, b_router, counts_in, tm, name):
    n, d = x.shape
    tm = min(tm, n)
    row = lambda i: (i, 0)
    fixed = lambda i: (0, 0)
    return pl.pallas_call(
        functools.partial(_router_body, tm=tm),
        out_shape=(jax.ShapeDtypeStruct((n, d), F32),
                   jax.ShapeDtypeStruct((n, TOP_K), I32),
                   jax.ShapeDtypeStruct((n, TOP_K), F32),
                   jax.ShapeDtypeStruct((n, TOP_K), I32),
                   jax.ShapeDtypeStruct((1, N_EXPERTS), I32)),
        grid=(n // tm,),
        in_specs=[pl.BlockSpec((tm, d), row), pl.BlockSpec((1, d), fixed),
                  pl.BlockSpec((d, N_EXPERTS), fixed), pl.BlockSpec((1, N_EXPERTS), fixed),
                  pl.BlockSpec((1, N_EXPERTS), fixed)],
        out_specs=(pl.BlockSpec((tm, d), row), pl.BlockSpec((tm, TOP_K), row), pl.BlockSpec((tm, TOP_K), row),
                   pl.BlockSpec((tm, TOP_K), row), pl.BlockSpec((1, N_EXPERTS), fixed)),
        scratch_shapes=[pltpu.VMEM((1, N_EXPERTS), F32)],
        compiler_params=_params(1),
        name=name,
    )(x, g.reshape(1, d), w_router, b_router.reshape(1, N_EXPERTS), counts_in)


def _dest_body(pstart_ref, ids_ref, rank_ref, dest_ref):
    ids = ids_ref[...]
    dest = rank_ref[...]
    for e in range(N_EXPERTS):
        dest = dest + jnp.where(ids == e, pstart_ref[e], 0)
    dest_ref[...] = dest


def assignment_dest(pstart, ids, rank, name):
    n_assign = ids.size
    shape = (n_assign // LANES, LANES)
    spec = pl.BlockSpec(shape, lambda i, ps: (0, 0))
    dest = pl.pallas_call(
        _dest_body,
        out_shape=jax.ShapeDtypeStruct(shape, I32),
        grid_spec=pltpu.PrefetchScalarGridSpec(num_scalar_prefetch=1, grid=(1,), in_specs=[spec, spec],
                                               out_specs=spec),
        compiler_params=_params(1),
        name=name,
    )(pstart, ids.reshape(shape), rank.reshape(shape))
    return dest.reshape(n_assign)


def _dispatch_body(fill_ref, cnt_ref, dest_ref, xp_ref, xs_ref, xbuf_hbm, zero_ref, sem, zsem, *, tc, n_p_tiles):
    def scatter_tile(x_ref):
        def issue(t, carry):
            for k in range(TOP_K):
                pltpu.make_async_copy(x_ref.at[pl.ds(t, 1)],
                                      xbuf_hbm.at[pl.ds(dest_ref[0, 0, t * TOP_K + k], 1)], sem).start()
            return carry

        lax.fori_loop(0, tc, issue, 0, unroll=4)

    @pl.when(pl.program_id(0) < n_p_tiles)
    def _():
        scatter_tile(xp_ref)

    @pl.when(pl.program_id(0) >= n_p_tiles)
    def _():
        scatter_tile(xs_ref)

    @pl.when(pl.program_id(0) == 0)
    def _():
        zero_ref[...] = jnp.zeros_like(zero_ref)

        def pad_copy(dst):
            return pltpu.make_async_copy(zero_ref.at[pl.ds(0, 1)], xbuf_hbm.at[pl.ds(dst, 1)], zsem)

        def tail_copy(c):
            return pltpu.make_async_copy(zero_ref, xbuf_hbm.at[pl.ds(pl.multiple_of(c * ZERO_ROWS, ZERO_ROWS),
                                                                     ZERO_ROWS)], zsem)

        first_tail = (fill_ref[N_EXPERTS - 1] + cnt_ref[N_EXPERTS - 1]) // ZERO_ROWS
        n_tail = xbuf_hbm.shape[0] // ZERO_ROWS

        def tail_start(c, carry):
            tail_copy(c).start()
            return carry

        def tail_wait(c, carry):
            tail_copy(c).wait()
            return carry

        lax.fori_loop(first_tail, n_tail, tail_start, 0)
        lax.fori_loop(first_tail, n_tail, tail_wait, 0)

        def per_expert(e, carry):
            start = fill_ref[e]
            n_pad = cnt_ref[e]

            def one(r, c2):
                pad_copy(start + r).start()
                return c2

            lax.fori_loop(0, n_pad, one, 0)

            def one_wait(r, c2):
                pad_copy(start + r).wait()
                return c2

            lax.fori_loop(0, n_pad, one_wait, 0)
            return carry

        lax.fori_loop(0, N_EXPERTS, per_expert, 0)

    for k in range(TOP_K):
        pltpu.make_async_copy(xp_ref, xbuf_hbm.at[pl.ds(0, tc)], sem).wait()


def dispatch(xn_p, xn_s, dest, pad_start, pad_count, n_rows, tc, name):
    (n_p, d), n_s = xn_p.shape, xn_s.shape[0]
    n_p_tiles, n_s_tiles = n_p // tc, n_s // tc
    assert n_p_tiles * tc == n_p and n_s_tiles * tc == n_s
    n_tiles = n_p_tiles + n_s_tiles
    return pl.pallas_call(
        functools.partial(_dispatch_body, tc=tc, n_p_tiles=n_p_tiles),
        out_shape=jax.ShapeDtypeStruct((n_rows, d), xn_p.dtype),
        grid_spec=pltpu.PrefetchScalarGridSpec(
            num_scalar_prefetch=2,
            grid=(n_tiles,),
            in_specs=[pl.BlockSpec((1, 1, tc * TOP_K), lambda i, a, b: (i, 0, 0), memory_space=pltpu.SMEM),
                      pl.BlockSpec((tc, d), lambda i, a, b: (jnp.minimum(i, n_p_tiles - 1), 0)),
                      pl.BlockSpec((tc, d), lambda i, a, b: (jnp.maximum(i - n_p_tiles, 0), 0))],
            out_specs=pl.BlockSpec(memory_space=pl.ANY),
            scratch_shapes=[pltpu.VMEM((ZERO_ROWS, d), xn_p.dtype), pltpu.SemaphoreType.DMA,
                            pltpu.SemaphoreType.DMA]),
        compiler_params=_params(1),
        name=name,
    )(pad_start, pad_count, dest.reshape(n_tiles, 1, tc * TOP_K), xn_p, xn_s)


def _expert_body(be_ref, nu_ref, x_ref, wg_ref, wu_ref, bg_ref, bu_ref, wd_ref, bd_ref, o_ref, xb_ref, acc_ref):
    j = pl.program_id(1)

    @pl.when(pl.program_id(0) < nu_ref[0])
    def _():
        @pl.when(j == 0)
        def _():
            xb_ref[...] = x_ref[...].astype(BF16)
            acc_ref[...] = jnp.broadcast_to(bd_ref[...], acc_ref.shape)

        xb = xb_ref[...]
        gate = jnp.dot(xb, wg_ref[...], preferred_element_type=F32) + bg_ref[...]
        up = jnp.dot(xb, wu_ref[...], preferred_element_type=F32) + bu_ref[...]
        gate = jnp.minimum(gate, SWIGLU_LIMIT)
        up = jnp.clip(up, -SWIGLU_LIMIT, SWIGLU_LIMIT)
        act = (up + 1.0) * (gate * jax.nn.sigmoid(SWIGLU_ALPHA * gate))
        acc_ref[...] += jnp.dot(act.astype(BF16), wd_ref[...], preferred_element_type=F32)

        @pl.when(j == pl.num_programs(1) - 1)
        def _():
            o_ref[...] = acc_ref[...]

    @pl.when((pl.program_id(0) >= nu_ref[0]) & (j == 0))
    def _():
        o_ref[...] = jnp.zeros_like(o_ref)


def expert_ffn(block_expert, n_used, xbuf, w_gate, w_up, b_gu, w_down, b_down, bs, tf, name):
    n_rows, d = xbuf.shape
    n_blocks = n_rows // bs
    nt = D_FF // tf

    def blk(b, nu):
        return jnp.minimum(b, nu[0] - 1)

    def ff(b, j, nu):
        return jnp.where(b < nu[0], j, nt - 1)

    return pl.pallas_call(
        _expert_body,
        out_shape=jax.ShapeDtypeStruct((n_rows, d), F32),
        grid_spec=pltpu.PrefetchScalarGridSpec(
            num_scalar_prefetch=2,
            grid=(n_blocks, nt),
            in_specs=[pl.BlockSpec((bs, d), lambda b, j, be, nu: (blk(b, nu), 0)),
                      pl.BlockSpec((None, d, tf), lambda b, j, be, nu: (be[blk(b, nu)], 0, ff(b, j, nu))),
                      pl.BlockSpec((None, d, tf), lambda b, j, be, nu: (be[blk(b, nu)], 0, ff(b, j, nu))),
                      pl.BlockSpec((None, 1, tf), lambda b, j, be, nu: (be[blk(b, nu)], 0, ff(b, j, nu))),
                      pl.BlockSpec((None, 1, tf), lambda b, j, be, nu: (be[blk(b, nu)], 0, nt + ff(b, j, nu))),
                      pl.BlockSpec((None, tf, d), lambda b, j, be, nu: (be[blk(b, nu)], ff(b, j, nu), 0)),
                      pl.BlockSpec((None, 1, d), lambda b, j, be, nu: (be[blk(b, nu)], 0, 0))],
            out_specs=pl.BlockSpec((bs, d), lambda b, j, be, nu: (b, 0)),
            scratch_shapes=[pltpu.VMEM((bs, d), BF16), pltpu.VMEM((bs, d), F32)]),
        compiler_params=_params(2),
        name=name,
    )(block_expert, n_used, xbuf, w_gate, w_up, b_gu.reshape(N_EXPERTS, 1, 2 * D_FF),
      b_gu.reshape(N_EXPERTS, 1, 2 * D_FF), w_down, b_down.reshape(N_EXPERTS, 1, d))


def _combine_body(dest_ref, next_ref, gates_ref, h_ref, g_ref, ybuf_hbm, o_ref, buf_ref, sem, *, tc):
    i = pl.program_id(0)
    slot = i % 2

    def issue_tile(idx_ref, s):
        def issue(t, carry):
            for k in range(TOP_K):
                pltpu.make_async_copy(ybuf_hbm.at[pl.ds(idx_ref[0, 0, t * TOP_K + k], 1)],
                                      buf_ref.at[s, k, pl.ds(t, 1)], sem.at[s]).start()
            return carry

        lax.fori_loop(0, tc, issue, 0, unroll=4)

    @pl.when(i == 0)
    def _():
        issue_tile(dest_ref, slot)

    @pl.when(i + 1 < pl.num_programs(0))
    def _():
        issue_tile(next_ref, 1 - slot)

    for k in range(TOP_K):
        pltpu.make_async_copy(ybuf_hbm.at[pl.ds(0, tc)], buf_ref.at[slot, k], sem.at[slot]).wait()

    gates = gates_ref[...]
    y = h_ref[...]
    for k in range(TOP_K):
        y = y + buf_ref[slot, k] * gates[:, k:k + 1]
    o_ref[...] = _rms_scale(y, g_ref[...])


def combine(dest, gates, h, g_final, ybuf, tc, name):
    n, d = h.shape
    tc = min(tc, n)
    n_tiles = n // tc
    dest3 = dest.reshape(n_tiles, 1, tc * TOP_K)
    return pl.pallas_call(
        functools.partial(_combine_body, tc=tc),
        out_shape=jax.ShapeDtypeStruct((n, d), F32),
        grid=(n_tiles,),
        in_specs=[pl.BlockSpec((1, 1, tc * TOP_K), lambda i: (i, 0, 0), memory_space=pltpu.SMEM),
                  pl.BlockSpec((1, 1, tc * TOP_K), lambda i: (jnp.minimum(i + 1, n_tiles - 1), 0, 0),
                               memory_space=pltpu.SMEM),
                  pl.BlockSpec((tc, TOP_K), lambda i: (i, 0)),
                  pl.BlockSpec((tc, d), lambda i: (i, 0)),
                  pl.BlockSpec((1, d), lambda i: (0, 0)),
                  pl.BlockSpec(memory_space=pl.ANY)],
        out_specs=pl.BlockSpec((tc, d), lambda i: (i, 0)),
        scratch_shapes=[pltpu.VMEM((2, TOP_K, tc, d), F32), pltpu.SemaphoreType.DMA((2,))],
        compiler_params=_params(1),
        name=name,
    )(dest3, dest3, gates, h, g_final.reshape(1, d), ybuf)


def moe_and_final_norm(h_p, h_s, g_ffn, w_router, b_router, w_gate, w_up, b_gu, w_down, b_down, g_final, *, bs,
                       tf):
    n_p, n_s = h_p.shape[0], h_s.shape[0]
    n_assign = (n_p + n_s) * TOP_K
    n_blocks = -(-n_assign // bs) + N_EXPERTS
    no_counts = jnp.zeros((1, N_EXPERTS), I32)
    xn_p, ids_p, gates_p, rank_p, counts_p = router(h_p, g_ffn, w_router, b_router, no_counts, 512, "router_p")
    xn_s, ids_s, gates_s, rank_s, counts = router(h_s, g_ffn, w_router, b_router, counts_p, 256, "router_s")
    counts = counts.reshape(N_EXPERTS)
    padded = (counts + bs - 1) // bs * bs
    pend = jnp.cumsum(padded)
    pstart = pend - padded
    n_used = (pend[-1:] // bs).astype(I32)
    block_row0 = jnp.arange(n_blocks, dtype=I32) * bs
    block_expert = jnp.minimum(jnp.sum((pend[None, :] <= block_row0[:, None]).astype(I32), axis=1), N_EXPERTS - 1)
    dest = assignment_dest(pstart.astype(I32), jnp.concatenate([ids_p, ids_s]), jnp.concatenate([rank_p, rank_s]),
                           "dest")
    dest_p, dest_s = dest[:n_p * TOP_K], dest[n_p * TOP_K:]
    xbuf = dispatch(xn_p, xn_s, dest, (pstart + counts).astype(I32), (padded - counts).astype(I32), n_blocks * bs,
                    min(256, n_s), "dispatch")
    ybuf = expert_ffn(block_expert, n_used, xbuf, w_gate, w_up, b_gu, w_down, b_down, bs, tf, "experts")
    y_p = combine(dest_p, gates_p, h_p, g_final, ybuf, 256, "combine_p")
    y_s = combine(dest_s, gates_s, h_s, g_final, ybuf, 128, "combine_s")
    return y_p, y_s


def _bias_rows(table):
    flipped = table[:, ::-1]
    left = BIAS_SPAN - 3 * REL_MAX
    right = BIAS_SPAN - left - (2 * REL_MAX + 1)
    return jnp.pad(flipped, ((0, 0), (left, right)), mode="edge").astype(F32)


def kernel(x_prompt, x_sample, cache_attn_k, cache_attn_v, state_pool, cache_mem_k, cache_mem_v, mem_prompt, g_mix,
           w_in, rel_table, pool_w, pool_scale, w_out, g_memkv, w_mk, w_mv, g_mem, w_mq, w_mo, g_ffn, w_router,
           b_router, w_gu, b_gu, w_down, b_down, g_final):
    bp, s, d = x_prompt.shape
    bsz, t, _ = x_sample.shape
    keep = min(ATT_WINDOW, s)

    def layer(a):
        assert a.shape[0] == 1
        return a.reshape(a.shape[1:])

    (g_mix, w_in, rel_table, pool_w, pool_scale, w_out, g_memkv, w_mk, w_mv, g_mem, w_mq, w_mo, g_ffn, w_router,
     b_router, w_gu, b_gu, w_down, b_down, cache_attn_k, cache_attn_v, state_pool, cache_mem_k, cache_mem_v) = map(
        layer, (g_mix, w_in, rel_table, pool_w, pool_scale, w_out, g_memkv, w_mk, w_mv, g_mem, w_mq, w_mo, g_ffn,
                w_router, b_router, w_gu, b_gu, w_down, b_down, cache_attn_k, cache_attn_v, state_pool, cache_mem_k,
                cache_mem_v))
    w_in_b = w_in.astype(BF16)
    w_out_b = w_out.astype(BF16)
    w_mk_b, w_mv_b = w_mk.astype(BF16), w_mv.astype(BF16)
    w_mq_b, w_mo_b = w_mq.astype(BF16), w_mo.astype(BF16)
    pool_w_b = pool_w.astype(BF16)
    bias_rows = _bias_rows(rel_table)
    w_gu_rows = w_gu.reshape(N_EXPERTS * d, 2 * D_FF)
    cast_gate, cast_up = CastJob(w_gu_rows, 0, D_FF), CastJob(w_gu_rows, 1, D_FF)
    cast_down = CastJob(w_down.reshape(N_EXPERTS * D_FF, d), 0, d)

    z, w_gate_b = norm_matmul(x_prompt.reshape(bp * s, d), g_mix, w_in_b, BF16, 1024, 1024, "in_proj_p", cast_gate)
    z = z.reshape(bp, s, IN_WIDTH)
    att = band_attention_prompt(z, bias_rows, 256)
    tm_o = 512
    h, w_up_b = outproj(att, z, 3, z, lambda bi, i, j: (bi, jnp.maximum(i * (tm_o // HIST_ROWS) - 1, 0), 3),
                        x_prompt, pool_w_b, pool_scale, w_out_b, 0, tm_o, 1024, True, "outproj_p", cast_up)
    mem2d = mem_prompt.reshape(bp * N_MEM, d)
    mk = norm_matmul(mem2d, g_memkv, w_mk_b, F32, 1024, 1024, "mem_k_p").reshape(bp, N_MEM, d)
    mv = norm_matmul(mem2d, g_memkv, w_mv_b, F32, 1024, 1024, "mem_v_p").reshape(bp, N_MEM, d)
    qm = norm_matmul(h.reshape(bp * s, d), g_mem, w_mq_b, BF16, 1024, 1024, "mem_q_p").reshape(bp, s, d)
    h, w_down_b = mem_attention(qm, mk, mv, h, w_mo_b, 512, 1024, "mem_attn_p", cast_down)
    w_gate_b = w_gate_b.reshape(N_EXPERTS, d, D_FF)
    w_up_b = w_up_b.reshape(N_EXPERTS, d, D_FF)
    w_down_b = w_down_b.reshape(N_EXPERTS, D_FF, d)

    new_k_p = z[:, s - keep:, A_WIDTH:2 * A_WIDTH].astype(F32).reshape(1, bp, keep, N_HEADS_A, HEAD_DIM_A)
    new_v_p = z[:, s - keep:, 2 * A_WIDTH:3 * A_WIDTH].astype(F32).reshape(1, bp, keep, N_HEADS_A, HEAD_DIM_A)
    new_pool_p = z[:, s - POOL_HIST:, 3 * A_WIDTH:].astype(F32)[None]
    new_mk_p = mk.reshape(1, bp, N_MEM, N_HEADS_MEM, HEAD_DIM_MEM)
    new_mv_p = mv.reshape(1, bp, N_MEM, N_HEADS_MEM, HEAD_DIM_MEM)

    zs = norm_matmul(x_sample.reshape(bsz * t, d), g_mix, w_in_b, BF16, 256, 1024, "in_proj_s")
    zs = zs.reshape(bsz, t, IN_WIDTH)
    w_cache = cache_attn_k.shape[1]
    att_s = band_attention_sample(zs, cache_attn_k.reshape(bsz, w_cache, A_WIDTH),
                                  cache_attn_v.reshape(bsz, w_cache, A_WIDTH), bias_rows)
    hist_s = jnp.pad(state_pool, ((0, 0), (HIST_ROWS - POOL_HIST, 0), (0, 0)))
    hs = outproj(att_s, zs, 3, hist_s, lambda bi, i, j: (bi, 0, 0), x_sample, pool_w_b, pool_scale, w_out_b,
                 PAST_LEN, t, 1024, False, "outproj_s")
    qs = norm_matmul(hs.reshape(bsz * t, d), g_mem, w_mq_b, BF16, 256, 1024, "mem_q_s").reshape(bsz, t, d)
    hs = mem_attention(qs, cache_mem_k.reshape(bsz, N_MEM, d), cache_mem_v.reshape(bsz, N_MEM, d), hs,
                       w_mo_b, t, 1024, "mem_attn_s")

    y_prompt, y_sample = moe_and_final_norm(h.reshape(bp * s, d), hs.reshape(bsz * t, d), g_ffn, w_router, b_router,
                                            w_gate_b, w_up_b, b_gu, w_down_b, b_down, g_final, bs=512, tf=512)
    y_prompt = y_prompt.reshape(bp, s, d)
    y_sample = y_sample.reshape(bsz, t, d)

    new_k_s = zs[:, :, A_WIDTH:2 * A_WIDTH].astype(F32).reshape(1, bsz, t, N_HEADS_A, HEAD_DIM_A)
    new_v_s = zs[:, :, 2 * A_WIDTH:3 * A_WIDTH].astype(F32).reshape(1, bsz, t, N_HEADS_A, HEAD_DIM_A)
    us_ext = jnp.concatenate([state_pool, zs[:, :, 3 * A_WIDTH:].astype(F32)], axis=1)
    new_pool_s = us_ext[:, -POOL_HIST:][None]

    return (y_prompt, y_sample, new_k_p, new_v_p, new_pool_p, new_mk_p, new_mv_p, new_k_s, new_v_s, new_pool_s)
```

```python
import functools
from typing import NamedTuple, Optional

import jax
import jax.numpy as jnp
from jax import lax
from jax.experimental import pallas as pl
from jax.experimental.pallas import tpu as pltpu

F32 = jnp.float32
BF16 = jnp.bfloat16
I32 = jnp.int32

D_MODEL = 2048
PAST_LEN = 4096
CHUNK = 64
N_LEFT_CHUNKS = 8
ATT_WINDOW = N_LEFT_CHUNKS * CHUNK
A_WIDTH = 1024
N_HEADS_A = 8
HEAD_DIM_A = A_WIDTH // N_HEADS_A
REL_MAX = 128
B_WIDTH = 1024
POOL_WINDOWS = (2, 4, 8, 16)
POOL_GROUP = B_WIDTH // len(POOL_WINDOWS)
POOL_HIST = max(POOL_WINDOWS) - 1
HIST_ROWS = POOL_HIST + 1
IN_WIDTH = 3 * A_WIDTH + B_WIDTH
N_MEM = 256
N_HEADS_MEM = 4
HEAD_DIM_MEM = D_MODEL // N_HEADS_MEM
N_EXPERTS = 32
TOP_K = 4
D_FF = D_MODEL
SWIGLU_LIMIT = 7.0
SWIGLU_ALPHA = 1.702
EPS = 1e-5
NEG_INF = -1e30

V7X_VMEM_LIMIT_BYTES = 56 * 1024 * 1024
LANES = 128
BIAS_SPAN = 1024
ZERO_ROWS = 128


def _params(n_grid):
    return pltpu.CompilerParams(dimension_semantics=("arbitrary",) * n_grid,
                                vmem_limit_bytes=V7X_VMEM_LIMIT_BYTES)


def _weight_spec(block_shape, index_map, resident):
    if resident:
        return pl.BlockSpec(block_shape, index_map, pipeline_mode=pl.Buffered(1))
    return pl.BlockSpec(block_shape, index_map)


def _rms_scale(x, g):
    return x * lax.rsqrt(jnp.mean(x * x, axis=-1, keepdims=True) + EPS) * g


class CastJob(NamedTuple):
    src: jax.Array
    col_block: int
    cols: int


def _call_with_cast_job(body, job, grid, in_specs, out_specs, out_shape, scratch_shapes, args, name):
    n_grid = len(grid)
    if job is None:
        return pl.pallas_call(body, out_shape=out_shape, grid=grid, in_specs=in_specs, out_specs=out_specs,
                              scratch_shapes=scratch_shapes, compiler_params=_params(n_grid), name=name)(*args)
    n_steps = 1
    for g in grid:
        n_steps *= g
    rows = job.src.shape[0]
    step_rows = rows // n_steps
    assert step_rows * n_steps == rows and step_rows % 16 == 0

    def flat_step(*idx):
        flat = idx[0]
        for g, i in zip(grid[1:], idx[1:]):
            flat = flat * g + i
        return flat

    n_in, n_out = len(in_specs), len(out_specs)

    def body_with_cast(*refs):
        ins, src_ref, rest = refs[:n_in], refs[n_in], refs[n_in + 1:]
        outs, dst_ref, scratch = rest[:n_out], rest[n_out], rest[n_out + 1:]
        body(*ins, *outs, *scratch)
        dst_ref[...] = src_ref[...].astype(BF16)

    return pl.pallas_call(
        body_with_cast,
        out_shape=tuple(out_shape) + (jax.ShapeDtypeStruct((rows, job.cols), BF16),),
        grid=grid,
        in_specs=list(in_specs) + [pl.BlockSpec((step_rows, job.cols), lambda *idx: (flat_step(*idx), job.col_block))],
        out_specs=tuple(out_specs) + (pl.BlockSpec((step_rows, job.cols), lambda *idx: (flat_step(*idx), 0)),),
        scratch_shapes=scratch_shapes,
        compiler_params=_params(n_grid),
        name=name,
    )(*args, job.src)


def _norm_matmul_body(x_ref, g_ref, w_ref, o_ref, xn_ref, *, scale, n_scaled_tiles):
    @pl.when(pl.program_id(1) == 0)
    def _():
        xn_ref[...] = _rms_scale(x_ref[...].astype(F32), g_ref[...]).astype(BF16)

    acc = jnp.dot(xn_ref[...], w_ref[...], preferred_element_type=F32)
    if scale is not None:
        acc = acc * jnp.where(pl.program_id(1) < n_scaled_tiles, scale, 1.0)
    o_ref[...] = acc.astype(o_ref.dtype)


def norm_matmul(x, g, w, out_dtype, tm, tn, name, cast_job=None, scale=None, scaled_cols=0):
    m, d = x.shape
    n = w.shape[1]
    tm, tn = min(tm, m), min(tn, n)
    assert scaled_cols % tn == 0
    out = _call_with_cast_job(
        functools.partial(_norm_matmul_body, scale=scale, n_scaled_tiles=scaled_cols // tn), cast_job,
        (m // tm, n // tn),
        [pl.BlockSpec((tm, d), lambda i, j: (i, 0)),
         pl.BlockSpec((1, d), lambda i, j: (0, 0)),
         pl.BlockSpec((d, tn), lambda i, j: (0, j))],
        (pl.BlockSpec((tm, tn), lambda i, j: (i, j)),),
        (jax.ShapeDtypeStruct((m, n), out_dtype),),
        [pltpu.VMEM((tm, d), BF16)],
        (x, g.reshape(1, d), w), name)
    return out[0] if cast_job is None else out


def _build_band_bias(g_ref, bias_ref, tq, tk):
    row = lax.broadcasted_iota(I32, (tq, tk), 0)
    col = lax.broadcasted_iota(I32, (tq, tk), 1)
    rel_chunk = col // CHUNK - N_LEFT_CHUNKS - row // CHUNK
    in_band = (rel_chunk <= 0) & (rel_chunk >= -N_LEFT_CHUNKS)
    lo = BIAS_SPAN - ATT_WINDOW - 2 * REL_MAX
    for h in range(N_HEADS_A):
        base = jnp.broadcast_to(g_ref[h:h + 1, :], (tq, BIAS_SPAN))
        rolled = pltpu.roll(base, 0, 1, stride=1, stride_axis=0)
        bias_ref[h] = jnp.where(in_band, rolled[:, lo:lo + tk], NEG_INF)


def _band_attn_body(*refs, n_kv, tq, kw, dynamic_first):
    g_ref, q_ref = refs[0], refs[1]
    k_refs = refs[2:2 + n_kv]
    v_refs = refs[2 + n_kv:2 + 2 * n_kv]
    o_ref, bias_ref = refs[2 + 2 * n_kv], refs[3 + 2 * n_kv]
    tk = sum(kw)
    first = (pl.program_id(0) == 0) & (pl.program_id(1) == 0)

    @pl.when(first)
    def _():
        _build_band_bias(g_ref, bias_ref, tq, tk)

    def all_heads(mask_missing_keys):
        if mask_missing_keys:
            col = lax.broadcasted_iota(I32, (tq, tk), 1)
            exists = col >= ATT_WINDOW - pl.program_id(1) * tq
        for h in range(N_HEADS_A):
            sl = slice(h * HEAD_DIM_A, (h + 1) * HEAD_DIM_A)
            q = q_ref[:, sl]
            s = jnp.concatenate(
                [lax.dot_general(q, k_ref[:, sl].astype(BF16), (((1,), (1,)), ((), ())),
                                 preferred_element_type=F32) for k_ref in k_refs], axis=1)
            s = s + bias_ref[h]
            if mask_missing_keys:
                s = jnp.where(exists, s, NEG_INF)
            m = jnp.max(s, axis=-1, keepdims=True)
            p = jnp.exp(s - m)
            l = jnp.sum(p, axis=-1, keepdims=True)
            pb = p.astype(BF16)
            o = None
            off = 0
            for v_ref, w in zip(v_refs, kw):
                part = jnp.dot(pb[:, off:off + w], v_ref[:, sl].astype(BF16), preferred_element_type=F32)
                o = part if o is None else o + part
                off += w
            o_ref[:, sl] = (o / l).astype(o_ref.dtype)

    if dynamic_first:
        some_missing = pl.program_id(1) * tq < ATT_WINDOW
        pl.when(some_missing)(lambda: all_heads(True))
        pl.when(jnp.logical_not(some_missing))(lambda: all_heads(False))
    else:
        all_heads(False)


def band_attention_prompt(z, bias_rows, tq, cast_job=None):
    b, s, _ = z.shape
    n_kv = ATT_WINDOW // tq + 1
    tk = n_kv * tq
    blk = (None, tq, A_WIDTH)

    def kv_spec(back, colblk):
        return pl.BlockSpec(blk, lambda bi, i: (bi, jnp.maximum(i - back, 0), colblk))

    in_specs = [pl.BlockSpec((N_HEADS_A, BIAS_SPAN), lambda bi, i: (0, 0)),
                pl.BlockSpec(blk, lambda bi, i: (bi, i, 0))]
    in_specs += [kv_spec(n_kv - 1 - j, 1) for j in range(n_kv)]
    in_specs += [kv_spec(n_kv - 1 - j, 2) for j in range(n_kv)]
    out = _call_with_cast_job(
        functools.partial(_band_attn_body, n_kv=n_kv, tq=tq, kw=(tq,) * n_kv, dynamic_first=True), cast_job,
        (b, s // tq), in_specs, (pl.BlockSpec(blk, lambda bi, i: (bi, i, 0)),),
        (jax.ShapeDtypeStruct((b, s, A_WIDTH), BF16),),
        [pltpu.VMEM((N_HEADS_A, tq, tk), F32)],
        (bias_rows,) + (z,) * (1 + 2 * n_kv), "band_attn_prompt")
    return out[0] if cast_job is None else out


def band_attention_sample(z, cache_k, cache_v, bias_rows):
    b, t, _ = z.shape
    w = cache_k.shape[1]
    assert w == ATT_WINDOW and PAST_LEN >= w and PAST_LEN % CHUNK == 0 and t <= CHUNK
    new_blk = (None, t, A_WIDTH)
    old_blk = (None, w, A_WIDTH)
    return pl.pallas_call(
        functools.partial(_band_attn_body, n_kv=2, tq=t, kw=(w, t), dynamic_first=False),
        out_shape=jax.ShapeDtypeStruct((b, t, A_WIDTH), BF16),
        grid=(b, 1),
        in_specs=[pl.BlockSpec((N_HEADS_A, BIAS_SPAN), lambda bi, i: (0, 0)),
                  pl.BlockSpec(new_blk, lambda bi, i: (bi, 0, 0)),
                  pl.BlockSpec(old_blk, lambda bi, i: (bi, 0, 0)),
                  pl.BlockSpec(new_blk, lambda bi, i: (bi, 0, 1)),
                  pl.BlockSpec(old_blk, lambda bi, i: (bi, 0, 0)),
                  pl.BlockSpec(new_blk, lambda bi, i: (bi, 0, 2))],
        out_specs=pl.BlockSpec(new_blk, lambda bi, i: (bi, 0, 0)),
        scratch_shapes=[pltpu.VMEM((N_HEADS_A, t, w + t), F32)],
        compiler_params=_params(2),
        name="band_attn_sample",
    )(bias_rows, z, cache_k, z, cache_v, z)


def _outproj_body(att_ref, u_ref, hist_ref, x_ref, wp_ref, ps_ref, w_ref, o_ref, cat_ref, *, pos0, tm,
                  zero_first_hist):
    i = pl.program_id(1)

    @pl.when(pl.program_id(2) == 0)
    def _():
        cat_ref[:, :A_WIDTH] = att_ref[...]
        hist = hist_ref[...].astype(F32)
        if zero_first_hist:
            hist = jnp.where(i > 0, hist, 0.0)
        ext = jnp.concatenate([hist, u_ref[...].astype(F32)], axis=0)
        pos = pos0 + i * tm + lax.broadcasted_iota(I32, (tm, 1), 0)
        for g, win in enumerate(POOL_WINDOWS):
            sl = slice(g * POOL_GROUP, (g + 1) * POOL_GROUP)
            e = ext[:, sl]
            run, span = e, 1
            while span < win:
                run = run[span:] + run[:-span]
                span *= 2
            first = HIST_ROWS - win + 1
            wsum = run[first:first + tm]
            cnt = jnp.minimum(pos + 1, win).astype(F32)
            d = (wsum / cnt - e[HIST_ROWS:]).astype(BF16)
            pool = jnp.dot(d, wp_ref[g], preferred_element_type=F32) * ps_ref[:, sl]
            cat_ref[:, A_WIDTH + g * POOL_GROUP:A_WIDTH + (g + 1) * POOL_GROUP] = pool.astype(BF16)

    o_ref[...] = x_ref[...] + jnp.dot(cat_ref[...], w_ref[...], preferred_element_type=F32)


def outproj(att, u_src, u_colblk, hist_src, hist_map, x, pool_w, pool_scale, w_out, pos0, tm, tn,
            zero_first_hist, name, cast_job=None):
    b, s, d = x.shape
    tm = min(tm, s)
    out = _call_with_cast_job(
        functools.partial(_outproj_body, pos0=pos0, tm=tm, zero_first_hist=zero_first_hist), cast_job,
        (b, s // tm, d // tn),
        [pl.BlockSpec((None, tm, A_WIDTH), lambda bi, i, j: (bi, i, 0)),
         pl.BlockSpec((None, tm, B_WIDTH), lambda bi, i, j: (bi, i, u_colblk)),
         pl.BlockSpec((None, HIST_ROWS, B_WIDTH), hist_map),
         pl.BlockSpec((None, tm, tn), lambda bi, i, j: (bi, i, j)),
         pl.BlockSpec((len(POOL_WINDOWS), POOL_GROUP, POOL_GROUP), lambda bi, i, j: (0, 0, 0)),
         pl.BlockSpec((1, B_WIDTH), lambda bi, i, j: (0, 0)),
         _weight_spec((A_WIDTH + B_WIDTH, tn), lambda bi, i, j: (0, j), resident=tn == d)],
        (pl.BlockSpec((None, tm, tn), lambda bi, i, j: (bi, i, j)),),
        (jax.ShapeDtypeStruct((b, s, d), F32),),
        [pltpu.VMEM((tm, A_WIDTH + B_WIDTH), BF16)],
        (att, u_src, hist_src, x, pool_w, pool_scale.reshape(1, B_WIDTH), w_out), name)
    return out[0] if cast_job is None else out


def _memattn_body(q_ref, mk_ref, mv_ref, x_ref, w_ref, o_ref, oh_ref):
    @pl.when(pl.program_id(2) == 0)
    def _():
        for h in range(N_HEADS_MEM):
            sl = slice(h * HEAD_DIM_MEM, (h + 1) * HEAD_DIM_MEM)
            s = lax.dot_general(q_ref[:, sl], mk_ref[:, sl].astype(BF16), (((1,), (1,)), ((), ())),
                                preferred_element_type=F32)
            p = jnp.exp(s - jnp.max(s, axis=-1, keepdims=True))
            l = jnp.sum(p, axis=-1, keepdims=True)
            o = jnp.dot(p.astype(BF16), mv_ref[:, sl].astype(BF16), preferred_element_type=F32) / l
            oh_ref[:, sl] = o.astype(BF16)

    o_ref[...] = x_ref[...] + jnp.dot(oh_ref[...], w_ref[...], preferred_element_type=F32)


def mem_attention(q, mk, mv, x, w_mo, tm, tn, name, cast_job=None):
    b, s, d = x.shape
    tm = min(tm, s)
    out = _call_with_cast_job(
        _memattn_body, cast_job, (b, s // tm, d // tn),
        [pl.BlockSpec((None, tm, d), lambda bi, i, j: (bi, i, 0)),
         pl.BlockSpec((None, N_MEM, d), lambda bi, i, j: (bi, 0, 0)),
         pl.BlockSpec((None, N_MEM, d), lambda bi, i, j: (bi, 0, 0)),
         pl.BlockSpec((None, tm, tn), lambda bi, i, j: (bi, i, j)),
         _weight_spec((d, tn), lambda bi, i, j: (0, j), resident=tn == d)],
        (pl.BlockSpec((None, tm, tn), lambda bi, i, j: (bi, i, j)),),
        (jax.ShapeDtypeStruct((b, s, d), F32),),
        [pltpu.VMEM((tm, d), BF16)],
        (q, mk, mv, x, w_mo), name)
    return out[0] if cast_job is None else out


def _router_body(x_ref, g_ref, w_ref, b_ref, cin_ref, xn_ref, ids_ref, gates_ref, rank_ref, cnt_ref, carry_ref, *,
                 tm):
    @pl.when(pl.program_id(0) == 0)
    def _():
        carry_ref[...] = cin_ref[...].astype(F32)

    xn = _rms_scale(x_ref[...], g_ref[...])
    xn_ref[...] = xn
    x_hi = xn.astype(BF16)
    x_lo = (xn - x_hi.astype(F32)).astype(BF16)
    w = w_ref[...]
    w_hi = w.astype(BF16)
    w_lo = (w - w_hi.astype(F32)).astype(BF16)
    logits = (jnp.dot(x_hi, w_hi, preferred_element_type=F32) + jnp.dot(x_lo, w_hi, preferred_element_type=F32)
              + jnp.dot(x_hi, w_lo, preferred_element_type=F32)) + b_ref[...]

    lane = lax.broadcasted_iota(I32, (tm, N_EXPERTS), 1)
    kcol = lax.broadcasted_iota(I32, (tm, TOP_K), 1)
    work = logits
    multi_hot = jnp.zeros((tm, N_EXPERTS), F32)
    ids = jnp.zeros((tm, TOP_K), I32)
    tops = jnp.zeros((tm, TOP_K), F32)
    picks = []
    for k in range(TOP_K):
        best = jnp.max(work, axis=-1, keepdims=True)
        idx = jnp.min(jnp.where(work == best, lane, N_EXPERTS), axis=-1, keepdims=True)
        hit = lane == idx
        picks.append(hit)
        multi_hot = multi_hot + hit.astype(F32)
        ids = jnp.where(kcol == k, idx, ids)
        tops = jnp.where(kcol == k, best, tops)
        work = jnp.where(hit, -jnp.inf, work)
    e = jnp.exp(tops - tops[:, 0:1])
    gates_ref[...] = e / jnp.sum(e, axis=-1, keepdims=True)
    ids_ref[...] = ids

    r = lax.broadcasted_iota(I32, (tm, tm), 0)
    c = lax.broadcasted_iota(I32, (tm, tm), 1)
    earlier = (c < r).astype(BF16)
    before = jnp.dot(earlier, multi_hot.astype(BF16), preferred_element_type=F32) + carry_ref[...]
    rank = jnp.zeros((tm, TOP_K), F32)
    for k in range(TOP_K):
        rk = jnp.sum(jnp.where(picks[k], before, 0.0), axis=-1, keepdims=True)
        rank = jnp.where(kcol == k, rk, rank)
    rank_ref[...] = rank.astype(I32)
    total = carry_ref[...] + jnp.sum(multi_hot, axis=0, keepdims=True)
    carry_ref[...] = total
    cnt_ref[...] = total.astype(I32)


def router(x, g, w_router, b_router, counts_in, tm, name):
    n, d = x.shape
    tm = min(tm, n)
    row = lambda i: (i, 0)
    fixed = lambda i: (0, 0)
    return pl.pallas_call(
        functools.partial(_router_body, tm=tm),
        out_shape=(jax.ShapeDtypeStruct((n, d), F32),
                   jax.ShapeDtypeStruct((n, TOP_K), I32),
                   jax.ShapeDtypeStruct((n, TOP_K), F32),
                   jax.ShapeDtypeStruct((n, TOP_K), I32),
                   jax.ShapeDtypeStruct((1, N_EXPERTS), I32)),
        grid=(n // tm,),
        in_specs=[pl.BlockSpec((tm, d), row), pl.BlockSpec((1, d), fixed),
                  pl.BlockSpec((d, N_EXPERTS), fixed), pl.BlockSpec((1, N_EXPERTS), fixed),
                  pl.BlockSpec((1, N_EXPERTS), fixed)],
        out_specs=(pl.BlockSpec((tm, d), row), pl.BlockSpec((tm, TOP_K), row), pl.BlockSpec((tm, TOP_K), row),
                   pl.BlockSpec((tm, TOP_K), row), pl.BlockSpec((1, N_EXPERTS), fixed)),
        scratch_shapes=[pltpu.VMEM((1, N_EXPERTS), F32)],
        compiler_params=_params(1),
        name=name,
    )(x, g.reshape(1, d), w_router, b_router.reshape(1, N_EXPERTS), counts_in)


def _dest_body(pstart_ref, ids_ref, rank_ref, dest_ref):
    ids = ids_ref[...]
    dest = rank_ref[...]
    for e in range(N_EXPERTS):
        dest = dest + jnp.where(ids == e, pstart_ref[e], 0)
    dest_ref[...] = dest


def assignment_dest(pstart, ids, rank, name):
    n_assign = ids.size
    shape = (n_assign // LANES, LANES)
    spec = pl.BlockSpec(shape, lambda i, ps: (0, 0))
    dest = pl.pallas_call(
        _dest_body,
        out_shape=jax.ShapeDtypeStruct(shape, I32),
        grid_spec=pltpu.PrefetchScalarGridSpec(num_scalar_prefetch=1, grid=(1,), in_specs=[spec, spec],
                                               out_specs=spec),
        compiler_params=_params(1),
        name=name,
    )(pstart, ids.reshape(shape), rank.reshape(shape))
    return dest.reshape(n_assign)


def _dispatch_body(fill_ref, cnt_ref, dest_ref, xp_ref, xs_ref, wsrc_ref, xbuf_hbm, wdst_ref, zero_ref, sem, zsem, *,
                   tc, n_p_tiles):
    wdst_ref[...] = wsrc_ref[...].astype(BF16)

    def scatter_tile(x_ref):
        def issue(t, carry):
            for k in range(TOP_K):
                pltpu.make_async_copy(x_ref.at[pl.ds(t, 1)],
                                      xbuf_hbm.at[pl.ds(dest_ref[0, 0, t * TOP_K + k], 1)], sem).start()
            return carry

        lax.fori_loop(0, tc, issue, 0, unroll=4)

    @pl.when(pl.program_id(0) < n_p_tiles)
    def _():
        scatter_tile(xp_ref)

    @pl.when(pl.program_id(0) >= n_p_tiles)
    def _():
        scatter_tile(xs_ref)

    @pl.when(pl.program_id(0) == 0)
    def _():
        zero_ref[...] = jnp.zeros_like(zero_ref)

        def pad_copy(dst):
            return pltpu.make_async_copy(zero_ref.at[pl.ds(0, 1)], xbuf_hbm.at[pl.ds(dst, 1)], zsem)

        def tail_copy(c):
            return pltpu.make_async_copy(zero_ref, xbuf_hbm.at[pl.ds(pl.multiple_of(c * ZERO_ROWS, ZERO_ROWS),
                                                                     ZERO_ROWS)], zsem)

        first_tail = (fill_ref[N_EXPERTS - 1] + cnt_ref[N_EXPERTS - 1]) // ZERO_ROWS
        n_tail = xbuf_hbm.shape[0] // ZERO_ROWS

        def tail_start(c, carry):
            tail_copy(c).start()
            return carry

        def tail_wait(c, carry):
            tail_copy(c).wait()
            return carry

        lax.fori_loop(first_tail, n_tail, tail_start, 0)
        lax.fori_loop(first_tail, n_tail, tail_wait, 0)

        def per_expert(e, carry):
            start = fill_ref[e]
            n_pad = cnt_ref[e]

            def one(r, c2):
                pad_copy(start + r).start()
                return c2

            lax.fori_loop(0, n_pad, one, 0)

            def one_wait(r, c2):
                pad_copy(start + r).wait()
                return c2

            lax.fori_loop(0, n_pad, one_wait, 0)
            return carry

        lax.fori_loop(0, N_EXPERTS, per_expert, 0)

    for k in range(TOP_K):
        pltpu.make_async_copy(xp_ref, xbuf_hbm.at[pl.ds(0, tc)], sem).wait()


def dispatch(xn_p, xn_s, dest, pad_start, pad_count, n_rows, tc, w_rows, name):
    (n_p, d), n_s = xn_p.shape, xn_s.shape[0]
    n_p_tiles, n_s_tiles = n_p // tc, n_s // tc
    assert n_p_tiles * tc == n_p and n_s_tiles * tc == n_s
    n_tiles = n_p_tiles + n_s_tiles
    w_step_rows = w_rows.shape[0] // n_p_tiles
    assert w_step_rows * n_p_tiles == w_rows.shape[0] and w_step_rows % 16 == 0
    w_spec = pl.BlockSpec((w_step_rows, w_rows.shape[1]), lambda i, a, b: (jnp.minimum(i, n_p_tiles - 1), 0))
    return pl.pallas_call(
        functools.partial(_dispatch_body, tc=tc, n_p_tiles=n_p_tiles),
        out_shape=(jax.ShapeDtypeStruct((n_rows, d), xn_p.dtype), jax.ShapeDtypeStruct(w_rows.shape, BF16)),
        grid_spec=pltpu.PrefetchScalarGridSpec(
            num_scalar_prefetch=2,
            grid=(n_tiles,),
            in_specs=[pl.BlockSpec((1, 1, tc * TOP_K), lambda i, a, b: (i, 0, 0), memory_space=pltpu.SMEM),
                      pl.BlockSpec((tc, d), lambda i, a, b: (jnp.minimum(i, n_p_tiles - 1), 0)),
                      pl.BlockSpec((tc, d), lambda i, a, b: (jnp.maximum(i - n_p_tiles, 0), 0)),
                      w_spec],
            out_specs=(pl.BlockSpec(memory_space=pl.ANY), w_spec),
            scratch_shapes=[pltpu.VMEM((ZERO_ROWS, d), xn_p.dtype), pltpu.SemaphoreType.DMA,
                            pltpu.SemaphoreType.DMA]),
        compiler_params=_params(1),
        name=name,
    )(pad_start, pad_count, dest.reshape(n_tiles, 1, tc * TOP_K), xn_p, xn_s, w_rows)


def _expert_body(be_ref, nu_ref, x_ref, wg_ref, wu_ref, bg_ref, bu_ref, wd_ref, bd_ref, o_ref, xb_ref):
    j = pl.program_id(1)

    @pl.when(pl.program_id(0) < nu_ref[0])
    def _():
        @pl.when(j == 0)
        def _():
            xb_ref[...] = x_ref[...].astype(BF16)
            o_ref[...] = jnp.broadcast_to(bd_ref[...], o_ref.shape)

        xb = xb_ref[...]
        gate = jnp.dot(xb, wg_ref[...], preferred_element_type=F32) + bg_ref[...]
        up = jnp.dot(xb, wu_ref[...], preferred_element_type=F32) + bu_ref[...]
        gate = jnp.minimum(gate, SWIGLU_LIMIT)
        up = jnp.clip(up, -SWIGLU_LIMIT, SWIGLU_LIMIT)
        act = (up + 1.0) * (gate * jax.nn.sigmoid(SWIGLU_ALPHA * gate))
        o_ref[...] += jnp.dot(act.astype(BF16), wd_ref[...], preferred_element_type=F32)

    @pl.when((pl.program_id(0) >= nu_ref[0]) & (j == 0))
    def _():
        o_ref[...] = jnp.zeros_like(o_ref)


def expert_ffn(block_expert, n_used, xbuf, w_gate, w_up, b_gu, w_down, b_down, bs, tf, name):
    n_rows, d = xbuf.shape
    n_blocks = n_rows // bs
    nt = D_FF // tf

    def blk(b, nu):
        return jnp.minimum(b, nu[0] - 1)

    def ff(b, j, nu):
        return jnp.where(b < nu[0], j, nt - 1)

    return pl.pallas_call(
        _expert_body,
        out_shape=jax.ShapeDtypeStruct((n_rows, d), F32),
        grid_spec=pltpu.PrefetchScalarGridSpec(
            num_scalar_prefetch=2,
            grid=(n_blocks, nt),
            in_specs=[pl.BlockSpec((bs, d), lambda b, j, be, nu: (blk(b, nu), 0)),
                      pl.BlockSpec((None, d, tf), lambda b, j, be, nu: (be[blk(b, nu)], 0, ff(b, j, nu))),
                      pl.BlockSpec((None, d, tf), lambda b, j, be, nu: (be[blk(b, nu)], 0, ff(b, j, nu))),
                      pl.BlockSpec((None, 1, tf), lambda b, j, be, nu: (be[blk(b, nu)], 0, ff(b, j, nu))),
                      pl.BlockSpec((None, 1, tf), lambda b, j, be, nu: (be[blk(b, nu)], 0, nt + ff(b, j, nu))),
                      pl.BlockSpec((None, tf, d), lambda b, j, be, nu: (be[blk(b, nu)], ff(b, j, nu), 0)),
                      pl.BlockSpec((None, 1, d), lambda b, j, be, nu: (be[blk(b, nu)], 0, 0))],
            out_specs=pl.BlockSpec((bs, d), lambda b, j, be, nu: (b, 0)),
            scratch_shapes=[pltpu.VMEM((bs, d), BF16)]),
        compiler_params=_params(2),
        name=name,
    )(block_expert, n_used, xbuf, w_gate, w_up, b_gu.reshape(N_EXPERTS, 1, 2 * D_FF),
      b_gu.reshape(N_EXPERTS, 1, 2 * D_FF), w_down, b_down.reshape(N_EXPERTS, 1, d))


def _combine_body(dest_ref, next_ref, gates_ref, h_ref, g_ref, ybuf_hbm, o_ref, buf_ref, sem, *, tc):
    i = pl.program_id(0)
    slot = i % 2

    def issue_tile(idx_ref, s):
        def issue(t, carry):
            for k in range(TOP_K):
                pltpu.make_async_copy(ybuf_hbm.at[pl.ds(idx_ref[0, 0, t * TOP_K + k], 1)],
                                      buf_ref.at[s, k, pl.ds(t, 1)], sem.at[s]).start()
            return carry

        lax.fori_loop(0, tc, issue, 0, unroll=4)

    @pl.when(i == 0)
    def _():
        issue_tile(dest_ref, slot)

    @pl.when(i + 1 < pl.num_programs(0))
    def _():
        issue_tile(next_ref, 1 - slot)

    for k in range(TOP_K):
        pltpu.make_async_copy(ybuf_hbm.at[pl.ds(0, tc)], buf_ref.at[slot, k], sem.at[slot]).wait()

    gates = gates_ref[...]
    y = h_ref[...]
    for k in range(TOP_K):
        y = y + buf_ref[slot, k] * gates[:, k:k + 1]
    o_ref[...] = _rms_scale(y, g_ref[...])


def combine(dest, gates, h, g_final, ybuf, tc, name):
    n, d = h.shape
    tc = min(tc, n)
    n_tiles = n // tc
    dest3 = dest.reshape(n_tiles, 1, tc * TOP_K)
    return pl.pallas_call(
        functools.partial(_combine_body, tc=tc),
        out_shape=jax.ShapeDtypeStruct((n, d), F32),
        grid=(n_tiles,),
        in_specs=[pl.BlockSpec((1, 1, tc * TOP_K), lambda i: (i, 0, 0), memory_space=pltpu.SMEM),
                  pl.BlockSpec((1, 1, tc * TOP_K), lambda i: (jnp.minimum(i + 1, n_tiles - 1), 0, 0),
                               memory_space=pltpu.SMEM),
                  pl.BlockSpec((tc, TOP_K), lambda i: (i, 0)),
                  pl.BlockSpec((tc, d), lambda i: (i, 0)),
                  pl.BlockSpec((1, d), lambda i: (0, 0)),
                  pl.BlockSpec(memory_space=pl.ANY)],
        out_specs=pl.BlockSpec((tc, d), lambda i: (i, 0)),
        scratch_shapes=[pltpu.VMEM((2, TOP_K, tc, d), F32), pltpu.SemaphoreType.DMA((2,))],
        compiler_params=_params(1),
        name=name,
    )(dest3, dest3, gates, h, g_final.reshape(1, d), ybuf)


def moe_and_final_norm(h_p, h_s, g_ffn, w_router, b_router, w_gate, w_up, b_gu, w_down, b_down, g_final, *, bs,
                       tf):
    n_p, n_s = h_p.shape[0], h_s.shape[0]
    n_assign = (n_p + n_s) * TOP_K
    n_blocks = -(-n_assign // bs) + N_EXPERTS
    no_counts = jnp.zeros((1, N_EXPERTS), I32)
    xn_p, ids_p, gates_p, rank_p, counts_p = router(h_p, g_ffn, w_router, b_router, no_counts, 512, "router_p")
    xn_s, ids_s, gates_s, rank_s, counts = router(h_s, g_ffn, w_router, b_router, counts_p, 256, "router_s")
    counts = counts.reshape(N_EXPERTS)
    padded = (counts + bs - 1) // bs * bs
    pend = jnp.cumsum(padded)
    pstart = pend - padded
    n_used = (pend[-1:] // bs).astype(I32)
    block_row0 = jnp.arange(n_blocks, dtype=I32) * bs
    block_expert = jnp.minimum(jnp.sum((pend[None, :] <= block_row0[:, None]).astype(I32), axis=1), N_EXPERTS - 1)
    dest = assignment_dest(pstart.astype(I32), jnp.concatenate([ids_p, ids_s]), jnp.concatenate([rank_p, rank_s]),
                           "dest")
    dest_p, dest_s = dest[:n_p * TOP_K], dest[n_p * TOP_K:]
    xbuf, w_down_b = dispatch(xn_p, xn_s, dest, (pstart + counts).astype(I32), (padded - counts).astype(I32),
                              n_blocks * bs, min(256, n_s), w_down.reshape(N_EXPERTS * D_FF, -1), "dispatch")
    ybuf = expert_ffn(block_expert, n_used, xbuf, w_gate, w_up, b_gu, w_down_b.reshape(w_down.shape), b_down, bs,
                      tf, "experts")
    y_p = combine(dest_p, gates_p, h_p, g_final, ybuf, 256, "combine_p")
    y_s = combine(dest_s, gates_s, h_s, g_final, ybuf, 128, "combine_s")
    return y_p, y_s


def _bias_rows(table):
    flipped = table[:, ::-1]
    left = BIAS_SPAN - 3 * REL_MAX
    right = BIAS_SPAN - left - (2 * REL_MAX + 1)
    return jnp.pad(flipped, ((0, 0), (left, right)), mode="edge").astype(F32)


def kernel(x_prompt, x_sample, cache_attn_k, cache_attn_v, state_pool, cache_mem_k, cache_mem_v, mem_prompt, g_mix,
           w_in, rel_table, pool_w, pool_scale, w_out, g_memkv, w_mk, w_mv, g_mem, w_mq, w_mo, g_ffn, w_router,
           b_router, w_gu, b_gu, w_down, b_down, g_final):
    bp, s, d = x_prompt.shape
    bsz, t, _ = x_sample.shape
    keep = min(ATT_WINDOW, s)

    def layer(a):
        assert a.shape[0] == 1
        return a.reshape(a.shape[1:])

    (g_mix, w_in, rel_table, pool_w, pool_scale, w_out, g_memkv, w_mk, w_mv, g_mem, w_mq, w_mo, g_ffn, w_router,
     b_router, w_gu, b_gu, w_down, b_down, cache_attn_k, cache_attn_v, state_pool, cache_mem_k, cache_mem_v) = map(
        layer, (g_mix, w_in, rel_table, pool_w, pool_scale, w_out, g_memkv, w_mk, w_mv, g_mem, w_mq, w_mo, g_ffn,
                w_router, b_router, w_gu, b_gu, w_down, b_down, cache_attn_k, cache_attn_v, state_pool, cache_mem_k,
                cache_mem_v))
    w_in_b = w_in.astype(BF16)
    w_out_b = w_out.astype(BF16)
    w_mk_b, w_mv_b = w_mk.astype(BF16), w_mv.astype(BF16)
    w_mq_b, w_mo_b = w_mq.astype(BF16), w_mo.astype(BF16)
    pool_w_b = pool_w.astype(BF16)
    bias_rows = _bias_rows(rel_table)
    w_gu_rows = w_gu.reshape(N_EXPERTS * d, 2 * D_FF)
    cast_gate, cast_up = CastJob(w_gu_rows, 0, D_FF), CastJob(w_gu_rows, 1, D_FF)

    att_scale, mem_scale = HEAD_DIM_A ** -0.5, HEAD_DIM_MEM ** -0.5
    z = norm_matmul(x_prompt.reshape(bp * s, d), g_mix, w_in_b, BF16, 1024, 1024, "in_proj_p",
                    scale=att_scale, scaled_cols=A_WIDTH)
    z = z.reshape(bp, s, IN_WIDTH)
    att, w_gate_b = band_attention_prompt(z, bias_rows, 256, cast_gate)
    tm_o = 512
    h, w_up_b = outproj(att, z, 3, z, lambda bi, i, j: (bi, jnp.maximum(i * (tm_o // HIST_ROWS) - 1, 0), 3),
                        x_prompt, pool_w_b, pool_scale, w_out_b, 0, tm_o, d, True, "outproj_p", cast_up)
    mem2d = mem_prompt.reshape(bp * N_MEM, d)
    mk = norm_matmul(mem2d, g_memkv, w_mk_b, F32, 1024, 1024, "mem_k_p").reshape(bp, N_MEM, d)
    mv = norm_matmul(mem2d, g_memkv, w_mv_b, F32, 1024, 1024, "mem_v_p").reshape(bp, N_MEM, d)
    qm = norm_matmul(h.reshape(bp * s, d), g_mem, w_mq_b, BF16, 1024, 1024, "mem_q_p", scale=mem_scale,
                     scaled_cols=d).reshape(bp, s, d)
    h = mem_attention(qm, mk.astype(BF16), mv.astype(BF16), h, w_mo_b, 512, d, "mem_attn_p")
    w_gate_b = w_gate_b.reshape(N_EXPERTS, d, D_FF)
    w_up_b = w_up_b.reshape(N_EXPERTS, d, D_FF)

    new_k_p = z[:, s - keep:, A_WIDTH:2 * A_WIDTH].astype(F32).reshape(1, bp, keep, N_HEADS_A, HEAD_DIM_A)
    new_v_p = z[:, s - keep:, 2 * A_WIDTH:3 * A_WIDTH].astype(F32).reshape(1, bp, keep, N_HEADS_A, HEAD_DIM_A)
    new_pool_p = z[:, s - POOL_HIST:, 3 * A_WIDTH:].astype(F32)[None]
    new_mk_p = mk.reshape(1, bp, N_MEM, N_HEADS_MEM, HEAD_DIM_MEM)
    new_mv_p = mv.reshape(1, bp, N_MEM, N_HEADS_MEM, HEAD_DIM_MEM)

    zs = norm_matmul(x_sample.reshape(bsz * t, d), g_mix, w_in_b, BF16, 256, 1024, "in_proj_s", scale=att_scale,
                     scaled_cols=A_WIDTH)
    zs = zs.reshape(bsz, t, IN_WIDTH)
    w_cache = cache_attn_k.shape[1]
    att_s = band_attention_sample(zs, cache_attn_k.reshape(bsz, w_cache, A_WIDTH),
                                  cache_attn_v.reshape(bsz, w_cache, A_WIDTH), bias_rows)
    hist_s = jnp.pad(state_pool, ((0, 0), (HIST_ROWS - POOL_HIST, 0), (0, 0)))
    hs = outproj(att_s, zs, 3, hist_s, lambda bi, i, j: (bi, 0, 0), x_sample, pool_w_b, pool_scale, w_out_b,
                 PAST_LEN, t, 1024, False, "outproj_s")
    qs = norm_matmul(hs.reshape(bsz * t, d), g_mem, w_mq_b, BF16, 256, 1024, "mem_q_s", scale=mem_scale,
                     scaled_cols=d).reshape(bsz, t, d)
    hs = mem_attention(qs, cache_mem_k.reshape(bsz, N_MEM, d), cache_mem_v.reshape(bsz, N_MEM, d), hs,
                       w_mo_b, t, 1024, "mem_attn_s")

    y_prompt, y_sample = moe_and_final_norm(h.reshape(bp * s, d), hs.reshape(bsz * t, d), g_ffn, w_router, b_router,
                                            w_gate_b, w_up_b, b_gu, w_down, b_down, g_final, bs=512, tf=1024)
    y_prompt = y_prompt.reshape(bp, s, d)
    y_sample = y_sample.reshape(bsz, t, d)

    new_k_s = zs[:, :, A_WIDTH:2 * A_WIDTH].astype(F32).reshape(1, bsz, t, N_HEADS_A, HEAD_DIM_A)
    new_v_s = zs[:, :, 2 * A_WIDTH:3 * A_WIDTH].astype(F32).reshape(1, bsz, t, N_HEADS_A, HEAD_DIM_A)
    us_ext = jnp.concatenate([state_pool, zs[:, :, 3 * A_WIDTH:].astype(F32)], axis=1)
    new_pool_s = us_ext[:, -POOL_HIST:][None]

    return (y_prompt, y_sample, new_k_p, new_v_p, new_pool_p, new_mk_p, new_mv_p, new_k_s, new_v_s, new_pool_s)
```

```python
import functools
from typing import NamedTuple, Optional

import jax
import jax.numpy as jnp
from jax import lax
from jax.experimental import pallas as pl
from jax.experimental.pallas import tpu as pltpu

F32 = jnp.float32
BF16 = jnp.bfloat16
I32 = jnp.int32

D_MODEL = 2048
PAST_LEN = 4096
CHUNK = 64
N_LEFT_CHUNKS = 8
ATT_WINDOW = N_LEFT_CHUNKS * CHUNK
A_WIDTH = 1024
N_HEADS_A = 8
HEAD_DIM_A = A_WIDTH // N_HEADS_A
REL_MAX = 128
B_WIDTH = 1024
POOL_WINDOWS = (2, 4, 8, 16)
POOL_GROUP = B_WIDTH // len(POOL_WINDOWS)
POOL_HIST = max(POOL_WINDOWS) - 1
HIST_ROWS = POOL_HIST + 1
IN_WIDTH = 3 * A_WIDTH + B_WIDTH
N_MEM = 256
N_HEADS_MEM = 4
HEAD_DIM_MEM = D_MODEL // N_HEADS_MEM
N_EXPERTS = 32
TOP_K = 4
D_FF = D_MODEL
SWIGLU_LIMIT = 7.0
SWIGLU_ALPHA = 1.702
EPS = 1e-5
NEG_INF = -1e30

V7X_VMEM_LIMIT_BYTES = 56 * 1024 * 1024
LANES = 128
BIAS_SPAN = 1024
ZERO_ROWS = 128
LOG2E = 1.4426950408889634


def _params(n_grid):
    return pltpu.CompilerParams(dimension_semantics=("arbitrary",) * n_grid,
                                vmem_limit_bytes=V7X_VMEM_LIMIT_BYTES)


def _weight_spec(block_shape, index_map, resident):
    if resident:
        return pl.BlockSpec(block_shape, index_map, pipeline_mode=pl.Buffered(1))
    return pl.BlockSpec(block_shape, index_map)


def _head_cols(ref, h, width):
    if len(ref.shape) == 3:
        return ref[:, h, :]
    return ref[:, h * width:(h + 1) * width]


def _rms_scale(x, g):
    return x * lax.rsqrt(jnp.mean(x * x, axis=-1, keepdims=True) + EPS) * g


class CastJob(NamedTuple):
    src: jax.Array
    col_block: int
    cols: int


def _call_with_cast_job(body, job, grid, in_specs, out_specs, out_shape, scratch_shapes, args, name):
    n_grid = len(grid)
    if job is None:
        return pl.pallas_call(body, out_shape=out_shape, grid=grid, in_specs=in_specs, out_specs=out_specs,
                              scratch_shapes=scratch_shapes, compiler_params=_params(n_grid), name=name)(*args)
    n_steps = 1
    for g in grid:
        n_steps *= g
    rows = job.src.shape[0]
    step_rows = rows // n_steps
    assert step_rows * n_steps == rows and step_rows % 16 == 0

    def flat_step(*idx):
        flat = idx[0]
        for g, i in zip(grid[1:], idx[1:]):
            flat = flat * g + i
        return flat

    n_in, n_out = len(in_specs), len(out_specs)

    def body_with_cast(*refs):
        ins, src_ref, rest = refs[:n_in], refs[n_in], refs[n_in + 1:]
        outs, dst_ref, scratch = rest[:n_out], rest[n_out], rest[n_out + 1:]
        body(*ins, *outs, *scratch)
        dst_ref[...] = src_ref[...].astype(BF16)

    return pl.pallas_call(
        body_with_cast,
        out_shape=tuple(out_shape) + (jax.ShapeDtypeStruct((rows, job.cols), BF16),),
        grid=grid,
        in_specs=list(in_specs) + [pl.BlockSpec((step_rows, job.cols), lambda *idx: (flat_step(*idx), job.col_block))],
        out_specs=tuple(out_specs) + (pl.BlockSpec((step_rows, job.cols), lambda *idx: (flat_step(*idx), 0)),),
        scratch_shapes=scratch_shapes,
        compiler_params=_params(n_grid),
        name=name,
    )(*args, job.src)


def _norm_matmul_body(x_ref, g_ref, w_ref, o_ref, xn_ref, *, scale, n_scaled_tiles):
    @pl.when(pl.program_id(1) == 0)
    def _():
        xn_ref[...] = _rms_scale(x_ref[...].astype(F32), g_ref[...]).astype(BF16)

    acc = jnp.dot(xn_ref[...], w_ref[...], preferred_element_type=F32)
    if scale is not None:
        acc = acc * jnp.where(pl.program_id(1) < n_scaled_tiles, scale, 1.0)
    o_ref[...] = acc.astype(o_ref.dtype)


def norm_matmul(x, g, w, out_dtype, tm, tn, name, cast_job=None, scale=None, scaled_cols=0):
    m, d = x.shape
    n = w.shape[1]
    tm, tn = min(tm, m), min(tn, n)
    assert scaled_cols % tn == 0
    out = _call_with_cast_job(
        functools.partial(_norm_matmul_body, scale=scale, n_scaled_tiles=scaled_cols // tn), cast_job,
        (m // tm, n // tn),
        [pl.BlockSpec((tm, d), lambda i, j: (i, 0)),
         pl.BlockSpec((1, d), lambda i, j: (0, 0)),
         pl.BlockSpec((d, tn), lambda i, j: (0, j))],
        (pl.BlockSpec((tm, tn), lambda i, j: (i, j)),),
        (jax.ShapeDtypeStruct((m, n), out_dtype),),
        [pltpu.VMEM((tm, d), BF16)],
        (x, g.reshape(1, d), w), name)
    return out[0] if cast_job is None else out


def _build_band_bias(g_ref, bias_ref, tq, tk):
    row = lax.broadcasted_iota(I32, (tq, tk), 0)
    col = lax.broadcasted_iota(I32, (tq, tk), 1)
    rel_chunk = col // CHUNK - N_LEFT_CHUNKS - row // CHUNK
    in_band = (rel_chunk <= 0) & (rel_chunk >= -N_LEFT_CHUNKS)
    lo = BIAS_SPAN - ATT_WINDOW - 2 * REL_MAX
    for h in range(N_HEADS_A):
        base = jnp.broadcast_to(g_ref[h:h + 1, :], (tq, BIAS_SPAN))
        rolled = pltpu.roll(base, 0, 1, stride=1, stride_axis=0)
        bias_ref[h] = jnp.where(in_band, rolled[:, lo:lo + tk], NEG_INF)


def _band_attn_scratch(tq, tk):
    return [pltpu.VMEM((N_HEADS_A, tq, tk), F32)]


def _band_attn_body(*refs, n_kv, tq, kw, dynamic_first):
    g_ref, q_ref = refs[0], refs[1]
    k_refs = refs[2:2 + n_kv]
    v_refs = refs[2 + n_kv:2 + 2 * n_kv]
    o_ref, bias_ref = refs[2 + 2 * n_kv], refs[3 + 2 * n_kv]
    tk = sum(kw)
    first = (pl.program_id(0) == 0) & (pl.program_id(1) == 0)

    @pl.when(first)
    def _():
        _build_band_bias(g_ref, bias_ref, tq, tk)

    def all_heads(mask_missing_keys):
        if mask_missing_keys:
            col = lax.broadcasted_iota(I32, (tq, tk), 1)
            exists = col >= ATT_WINDOW - pl.program_id(1) * tq
        for h in range(N_HEADS_A):
            sl = slice(h * HEAD_DIM_A, (h + 1) * HEAD_DIM_A)
            q = q_ref[:, sl]
            s = jnp.concatenate(
                [lax.dot_general(q, _head_cols(k_ref, h, HEAD_DIM_A).astype(BF16), (((1,), (1,)), ((), ())),
                                 preferred_element_type=F32) for k_ref in k_refs], axis=1)
            s = s + bias_ref[h]
            if mask_missing_keys:
                s = jnp.where(exists, s, NEG_INF)
            p = jnp.exp2(s - jnp.max(s, axis=-1, keepdims=True))
            l = jnp.sum(p, axis=-1, keepdims=True)
            pb = p.astype(BF16)
            o = None
            off = 0
            for v_ref, w in zip(v_refs, kw):
                part = jnp.dot(pb[:, off:off + w], _head_cols(v_ref, h, HEAD_DIM_A).astype(BF16),
                               preferred_element_type=F32)
                o = part if o is None else o + part
                off += w
            o_ref[:, sl] = (o / l).astype(o_ref.dtype)

    if dynamic_first:
        some_missing = pl.program_id(1) * tq < ATT_WINDOW
        pl.when(some_missing)(lambda: all_heads(True))
        pl.when(jnp.logical_not(some_missing))(lambda: all_heads(False))
    else:
        all_heads(False)


def band_attention_prompt(z, bias_rows, tq, cast_job=None):
    b, s, _ = z.shape
    n_kv = ATT_WINDOW // tq + 1
    tk = n_kv * tq
    blk = (None, tq, A_WIDTH)

    def kv_spec(back, colblk):
        return pl.BlockSpec(blk, lambda bi, i: (bi, jnp.maximum(i - back, 0), colblk))

    in_specs = [pl.BlockSpec((N_HEADS_A, BIAS_SPAN), lambda bi, i: (0, 0)),
                pl.BlockSpec(blk, lambda bi, i: (bi, i, 0))]
    in_specs += [kv_spec(n_kv - 1 - j, 1) for j in range(n_kv)]
    in_specs += [kv_spec(n_kv - 1 - j, 2) for j in range(n_kv)]
    out = _call_with_cast_job(
        functools.partial(_band_attn_body, n_kv=n_kv, tq=tq, kw=(tq,) * n_kv, dynamic_first=True), cast_job,
        (b, s // tq), in_specs, (pl.BlockSpec(blk, lambda bi, i: (bi, i, 0)),),
        (jax.ShapeDtypeStruct((b, s, A_WIDTH), BF16),),
        _band_attn_scratch(tq, tk),
        (bias_rows,) + (z,) * (1 + 2 * n_kv), "band_attn_prompt")
    return out[0] if cast_job is None else out


def band_attention_sample(z, cache_k, cache_v, bias_rows):
    b, t, _ = z.shape
    w = cache_k.shape[1]
    assert w == ATT_WINDOW and PAST_LEN >= w and PAST_LEN % CHUNK == 0 and t <= CHUNK
    new_blk = (None, t, A_WIDTH)
    old_blk = (None, w, N_HEADS_A, HEAD_DIM_A)
    return pl.pallas_call(
        functools.partial(_band_attn_body, n_kv=2, tq=t, kw=(w, t), dynamic_first=False),
        out_shape=jax.ShapeDtypeStruct((b, t, A_WIDTH), BF16),
        grid=(b, 1),
        in_specs=[pl.BlockSpec((N_HEADS_A, BIAS_SPAN), lambda bi, i: (0, 0)),
                  pl.BlockSpec(new_blk, lambda bi, i: (bi, 0, 0)),
                  pl.BlockSpec(old_blk, lambda bi, i: (bi, 0, 0, 0)),
                  pl.BlockSpec(new_blk, lambda bi, i: (bi, 0, 1)),
                  pl.BlockSpec(old_blk, lambda bi, i: (bi, 0, 0, 0)),
                  pl.BlockSpec(new_blk, lambda bi, i: (bi, 0, 2))],
        out_specs=pl.BlockSpec(new_blk, lambda bi, i: (bi, 0, 0)),
        scratch_shapes=_band_attn_scratch(t, w + t),
        compiler_params=_params(2),
        name="band_attn_sample",
    )(bias_rows, z, cache_k, z, cache_v, z)


def _outproj_body(att_ref, u_ref, hist_ref, x_ref, wp_ref, ps_ref, w_ref, o_ref, cat_ref, *, pos0, tm,
                  zero_first_hist):
    i = pl.program_id(1)

    @pl.when(pl.program_id(2) == 0)
    def _():
        cat_ref[:, :A_WIDTH] = att_ref[...]
        hist = hist_ref[...].astype(F32)
        if zero_first_hist:
            hist = jnp.where(i > 0, hist, 0.0)
        ext = jnp.concatenate([hist, u_ref[...].astype(F32)], axis=0)
        pos = pos0 + i * tm + lax.broadcasted_iota(I32, (tm, 1), 0)
        for g, win in enumerate(POOL_WINDOWS):
            sl = slice(g * POOL_GROUP, (g + 1) * POOL_GROUP)
            e = ext[:, sl]
            run, span = e, 1
            while span < win:
                run = run[span:] + run[:-span]
                span *= 2
            first = HIST_ROWS - win + 1
            wsum = run[first:first + tm]
            cnt = jnp.minimum(pos + 1, win).astype(F32)
            d = (wsum / cnt - e[HIST_ROWS:]).astype(BF16)
            pool = jnp.dot(d, wp_ref[g], preferred_element_type=F32) * ps_ref[:, sl]
            cat_ref[:, A_WIDTH + g * POOL_GROUP:A_WIDTH + (g + 1) * POOL_GROUP] = pool.astype(BF16)

    o_ref[...] = x_ref[...] + jnp.dot(cat_ref[...], w_ref[...], preferred_element_type=F32)


def outproj(att, u_src, u_colblk, hist_src, hist_map, x, pool_w, pool_scale, w_out, pos0, tm, tn,
            zero_first_hist, name, cast_job=None):
    b, s, d = x.shape
    tm = min(tm, s)
    out = _call_with_cast_job(
        functools.partial(_outproj_body, pos0=pos0, tm=tm, zero_first_hist=zero_first_hist), cast_job,
        (b, s // tm, d // tn),
        [pl.BlockSpec((None, tm, A_WIDTH), lambda bi, i, j: (bi, i, 0)),
         pl.BlockSpec((None, tm, B_WIDTH), lambda bi, i, j: (bi, i, u_colblk)),
         pl.BlockSpec((None, HIST_ROWS, B_WIDTH), hist_map),
         pl.BlockSpec((None, tm, tn), lambda bi, i, j: (bi, i, j)),
         pl.BlockSpec((len(POOL_WINDOWS), POOL_GROUP, POOL_GROUP), lambda bi, i, j: (0, 0, 0)),
         pl.BlockSpec((1, B_WIDTH), lambda bi, i, j: (0, 0)),
         _weight_spec((A_WIDTH + B_WIDTH, tn), lambda bi, i, j: (0, j), resident=tn == d)],
        (pl.BlockSpec((None, tm, tn), lambda bi, i, j: (bi, i, j)),),
        (jax.ShapeDtypeStruct((b, s, d), F32),),
        [pltpu.VMEM((tm, A_WIDTH + B_WIDTH), BF16)],
        (att, u_src, hist_src, x, pool_w, pool_scale.reshape(1, B_WIDTH), w_out), name)
    return out[0] if cast_job is None else out


def _memattn_body(q_ref, mk_ref, mv_ref, x_ref, w_ref, o_ref, oh_ref):
    @pl.when(pl.program_id(2) == 0)
    def _():
        for h in range(N_HEADS_MEM):
            sl = slice(h * HEAD_DIM_MEM, (h + 1) * HEAD_DIM_MEM)
            s = lax.dot_general(q_ref[:, sl], _head_cols(mk_ref, h, HEAD_DIM_MEM).astype(BF16),
                                (((1,), (1,)), ((), ())), preferred_element_type=F32)
            p = jnp.exp(s - jnp.max(s, axis=-1, keepdims=True))
            l = jnp.sum(p, axis=-1, keepdims=True)
            o = jnp.dot(p.astype(BF16), _head_cols(mv_ref, h, HEAD_DIM_MEM).astype(BF16),
                        preferred_element_type=F32) / l
            oh_ref[:, sl] = o.astype(BF16)

    o_ref[...] = x_ref[...] + jnp.dot(oh_ref[...], w_ref[...], preferred_element_type=F32)


def mem_attention(q, mk, mv, x, w_mo, tm, tn, name, cast_job=None):
    b, s, d = x.shape
    tm = min(tm, s)
    kv_spec = pl.BlockSpec((None,) + mk.shape[1:], lambda bi, i, j: (bi,) + (0,) * (mk.ndim - 1))
    out = _call_with_cast_job(
        _memattn_body, cast_job, (b, s // tm, d // tn),
        [pl.BlockSpec((None, tm, d), lambda bi, i, j: (bi, i, 0)),
         kv_spec,
         kv_spec,
         pl.BlockSpec((None, tm, tn), lambda bi, i, j: (bi, i, j)),
         _weight_spec((d, tn), lambda bi, i, j: (0, j), resident=tn == d)],
        (pl.BlockSpec((None, tm, tn), lambda bi, i, j: (bi, i, j)),),
        (jax.ShapeDtypeStruct((b, s, d), F32),),
        [pltpu.VMEM((tm, d), BF16)],
        (q, mk, mv, x, w_mo), name)
    return out[0] if cast_job is None else out


def _router_body(x_ref, g_ref, w_ref, b_ref, cin_ref, xn_ref, ids_ref, gates_ref, rank_ref, cnt_ref, carry_ref, *,
                 tm):
    @pl.when(pl.program_id(0) == 0)
    def _():
        carry_ref[...] = cin_ref[...].astype(F32)

    xn = _rms_scale(x_ref[...], g_ref[...])
    xn_ref[...] = xn
    x_hi = xn.astype(BF16)
    x_lo = (xn - x_hi.astype(F32)).astype(BF16)
    w = w_ref[...]
    w_hi = w.astype(BF16)
    w_lo = (w - w_hi.astype(F32)).astype(BF16)
    prod = jnp.dot(jnp.concatenate([x_hi, x_lo], axis=0), jnp.concatenate([w_hi, w_lo], axis=1),
                   preferred_element_type=F32)
    logits = (prod[:tm, :N_EXPERTS] + prod[tm:, :N_EXPERTS] + prod[:tm, N_EXPERTS:]) + b_ref[...]

    lane = lax.broadcasted_iota(I32, (tm, N_EXPERTS), 1)
    kcol = lax.broadcasted_iota(I32, (tm, TOP_K), 1)
    work = logits
    multi_hot = jnp.zeros((tm, N_EXPERTS), F32)
    ids = jnp.zeros((tm, TOP_K), I32)
    tops = jnp.zeros((tm, TOP_K), F32)
    picks = []
    for k in range(TOP_K):
        best = jnp.max(work, axis=-1, keepdims=True)
        idx = jnp.min(jnp.where(work == best, lane, N_EXPERTS), axis=-1, keepdims=True)
        hit = lane == idx
        picks.append(hit)
        multi_hot = multi_hot + hit.astype(F32)
        ids = jnp.where(kcol == k, idx, ids)
        tops = jnp.where(kcol == k, best, tops)
        work = jnp.where(hit, -jnp.inf, work)
    e = jnp.exp(tops - tops[:, 0:1])
    gates_ref[...] = e / jnp.sum(e, axis=-1, keepdims=True)
    ids_ref[...] = ids

    r = lax.broadcasted_iota(I32, (tm, tm), 0)
    c = lax.broadcasted_iota(I32, (tm, tm), 1)
    earlier = (c < r).astype(BF16)
    before = jnp.dot(earlier, multi_hot.astype(BF16), preferred_element_type=F32) + carry_ref[...]
    rank = jnp.zeros((tm, TOP_K), F32)
    for k in range(TOP_K):
        rk = jnp.sum(jnp.where(picks[k], before, 0.0), axis=-1, keepdims=True)
        rank = jnp.where(kcol == k, rk, rank)
    rank_ref[...] = rank.astype(I32)
    total = carry_ref[...] + jnp.sum(multi_hot, axis=0, keepdims=True)
    carry_ref[...] = total
    cnt_ref[...] = total.astype(I32)


def router(x, g, w_router, b_router, counts_in, tm, name):
    n, d = x.shape
    tm = min(tm, n)
    row = lambda i: (i, 0)
    fixed = lambda i: (0, 0)
    return pl.pallas_call(
        functools.partial(_router_body, tm=tm),
        out_shape=(jax.ShapeDtypeStruct((n, d), F32),
                   jax.ShapeDtypeStruct((n, TOP_K), I32),
                   jax.ShapeDtypeStruct((n, TOP_K), F32),
                   jax.ShapeDtypeStruct((n, TOP_K), I32),
                   jax.ShapeDtypeStruct((1, N_EXPERTS), I32)),
        grid=(n // tm,),
        in_specs=[pl.BlockSpec((tm, d), row), pl.BlockSpec((1, d), fixed),
                  pl.BlockSpec((d, N_EXPERTS), fixed), pl.BlockSpec((1, N_EXPERTS), fixed),
                  pl.BlockSpec((1, N_EXPERTS), fixed)],
        out_specs=(pl.BlockSpec((tm, d), row), pl.BlockSpec((tm, TOP_K), row), pl.BlockSpec((tm, TOP_K), row),
                   pl.BlockSpec((tm, TOP_K), row), pl.BlockSpec((1, N_EXPERTS), fixed)),
        scratch_shapes=[pltpu.VMEM((1, N_EXPERTS), F32)],
        compiler_params=_params(1),
        name=name,
    )(x, g.reshape(1, d), w_router, b_router.reshape(1, N_EXPERTS), counts_in)


def _dest_body(pstart_ref, ids_ref, rank_ref, dest_ref):
    ids = ids_ref[...]
    dest = rank_ref[...]
    for e in range(N_EXPERTS):
        dest = dest + jnp.where(ids == e, pstart_ref[e], 0)
    dest_ref[...] = dest


def assignment_dest(pstart, ids, rank, name):
    n_assign = ids.size
    shape = (n_assign // LANES, LANES)
    spec = pl.BlockSpec(shape, lambda i, ps: (0, 0))
    dest = pl.pallas_call(
        _dest_body,
        out_shape=jax.ShapeDtypeStruct(shape, I32),
        grid_spec=pltpu.PrefetchScalarGridSpec(num_scalar_prefetch=1, grid=(1,), in_specs=[spec, spec],
                                               out_specs=spec),
        compiler_params=_params(1),
        name=name,
    )(pstart, ids.reshape(shape), rank.reshape(shape))
    return dest.reshape(n_assign)


def _dispatch_body(fill_ref, cnt_ref, dest_ref, xp_ref, xs_ref, wsrc_ref, xbuf_hbm, wdst_ref, zero_ref, sem, zsem, *,
                   tc, n_p_tiles):
    wdst_ref[...] = wsrc_ref[...].astype(BF16)

    def scatter_tile(x_ref):
        def issue(t, carry):
            for k in range(TOP_K):
                pltpu.make_async_copy(x_ref.at[pl.ds(t, 1)],
                                      xbuf_hbm.at[pl.ds(dest_ref[0, 0, t * TOP_K + k], 1)], sem).start()
            return carry

        lax.fori_loop(0, tc, issue, 0, unroll=8)

    @pl.when(pl.program_id(0) < n_p_tiles)
    def _():
        scatter_tile(xp_ref)

    @pl.when(pl.program_id(0) >= n_p_tiles)
    def _():
        scatter_tile(xs_ref)

    @pl.when(pl.program_id(0) == 0)
    def _():
        zero_ref[...] = jnp.zeros_like(zero_ref)

        def pad_copy(dst):
            return pltpu.make_async_copy(zero_ref.at[pl.ds(0, 1)], xbuf_hbm.at[pl.ds(dst, 1)], zsem)

        def tail_copy(c):
            return pltpu.make_async_copy(zero_ref, xbuf_hbm.at[pl.ds(pl.multiple_of(c * ZERO_ROWS, ZERO_ROWS),
                                                                     ZERO_ROWS)], zsem)

        first_tail = (fill_ref[N_EXPERTS - 1] + cnt_ref[N_EXPERTS - 1]) // ZERO_ROWS
        n_tail = xbuf_hbm.shape[0] // ZERO_ROWS

        def tail_start(c, carry):
            tail_copy(c).start()
            return carry

        def tail_wait(c, carry):
            tail_copy(c).wait()
            return carry

        lax.fori_loop(first_tail, n_tail, tail_start, 0)
        lax.fori_loop(first_tail, n_tail, tail_wait, 0)

        def per_expert(e, carry):
            start = fill_ref[e]
            n_pad = cnt_ref[e]

            def one(r, c2):
                pad_copy(start + r).start()
                return c2

            lax.fori_loop(0, n_pad, one, 0)

            def one_wait(r, c2):
                pad_copy(start + r).wait()
                return c2

            lax.fori_loop(0, n_pad, one_wait, 0)
            return carry

        lax.fori_loop(0, N_EXPERTS, per_expert, 0)

    for k in range(TOP_K):
        pltpu.make_async_copy(xp_ref, xbuf_hbm.at[pl.ds(0, tc)], sem).wait()


def dispatch(xn_p, xn_s, dest, pad_start, pad_count, n_rows, tc, w_rows, name):
    (n_p, d), n_s = xn_p.shape, xn_s.shape[0]
    n_p_tiles, n_s_tiles = n_p // tc, n_s // tc
    assert n_p_tiles * tc == n_p and n_s_tiles * tc == n_s
    n_tiles = n_p_tiles + n_s_tiles
    w_step_rows = w_rows.shape[0] // n_p_tiles
    assert w_step_rows * n_p_tiles == w_rows.shape[0] and w_step_rows % 16 == 0
    w_spec = pl.BlockSpec((w_step_rows, w_rows.shape[1]), lambda i, a, b: (jnp.minimum(i, n_p_tiles - 1), 0))
    return pl.pallas_call(
        functools.partial(_dispatch_body, tc=tc, n_p_tiles=n_p_tiles),
        out_shape=(jax.ShapeDtypeStruct((n_rows, d), xn_p.dtype), jax.ShapeDtypeStruct(w_rows.shape, BF16)),
        grid_spec=pltpu.PrefetchScalarGridSpec(
            num_scalar_prefetch=2,
            grid=(n_tiles,),
            in_specs=[pl.BlockSpec((1, 1, tc * TOP_K), lambda i, a, b: (i, 0, 0), memory_space=pltpu.SMEM),
                      pl.BlockSpec((tc, d), lambda i, a, b: (jnp.minimum(i, n_p_tiles - 1), 0)),
                      pl.BlockSpec((tc, d), lambda i, a, b: (jnp.maximum(i - n_p_tiles, 0), 0)),
                      w_spec],
            out_specs=(pl.BlockSpec(memory_space=pl.ANY), w_spec),
            scratch_shapes=[pltpu.VMEM((ZERO_ROWS, d), xn_p.dtype), pltpu.SemaphoreType.DMA,
                            pltpu.SemaphoreType.DMA]),
        compiler_params=_params(1),
        name=name,
    )(pad_start, pad_count, dest.reshape(n_tiles, 1, tc * TOP_K), xn_p, xn_s, w_rows)


def _expert_body(be_ref, nu_ref, x_ref, wg_ref, wu_ref, bg_ref, bu_ref, wd_ref, bd_ref, o_ref, xb_ref):
    j = pl.program_id(1)

    @pl.when(pl.program_id(0) < nu_ref[0])
    def _():
        @pl.when(j == 0)
        def _():
            xb_ref[...] = x_ref[...].astype(BF16)
            o_ref[...] = jnp.broadcast_to(bd_ref[...], o_ref.shape)

        xb = xb_ref[...]
        gate = jnp.dot(xb, wg_ref[...], preferred_element_type=F32) + bg_ref[...]
        up = jnp.dot(xb, wu_ref[...], preferred_element_type=F32) + bu_ref[...]
        gate = jnp.minimum(gate, SWIGLU_LIMIT)
        up = jnp.clip(up, -SWIGLU_LIMIT, SWIGLU_LIMIT)
        act = (up + 1.0) * (gate * jax.nn.sigmoid(SWIGLU_ALPHA * gate))
        o_ref[...] += jnp.dot(act.astype(BF16), wd_ref[...], preferred_element_type=F32)

    @pl.when((pl.program_id(0) >= nu_ref[0]) & (j == 0))
    def _():
        o_ref[...] = jnp.zeros_like(o_ref)


def expert_ffn(block_expert, n_used, xbuf, w_gate, w_up, b_gu, w_down, b_down, bs, tf, name):
    n_rows, d = xbuf.shape
    n_blocks = n_rows // bs
    nt = D_FF // tf

    def blk(b, nu):
        return jnp.minimum(b, nu[0] - 1)

    def ff(b, j, nu):
        return jnp.where(b < nu[0], j, nt - 1)

    return pl.pallas_call(
        _expert_body,
        out_shape=jax.ShapeDtypeStruct((n_rows, d), F32),
        grid_spec=pltpu.PrefetchScalarGridSpec(
            num_scalar_prefetch=2,
            grid=(n_blocks, nt),
            in_specs=[pl.BlockSpec((bs, d), lambda b, j, be, nu: (blk(b, nu), 0)),
                      pl.BlockSpec((None, d, tf), lambda b, j, be, nu: (be[blk(b, nu)], 0, ff(b, j, nu))),
                      pl.BlockSpec((None, d, tf), lambda b, j, be, nu: (be[blk(b, nu)], 0, ff(b, j, nu))),
                      pl.BlockSpec((None, 1, tf), lambda b, j, be, nu: (be[blk(b, nu)], 0, ff(b, j, nu))),
                      pl.BlockSpec((None, 1, tf), lambda b, j, be, nu: (be[blk(b, nu)], 0, nt + ff(b, j, nu))),
                      pl.BlockSpec((None, tf, d), lambda b, j, be, nu: (be[blk(b, nu)], ff(b, j, nu), 0)),
                      pl.BlockSpec((None, 1, d), lambda b, j, be, nu: (be[blk(b, nu)], 0, 0))],
            out_specs=pl.BlockSpec((bs, d), lambda b, j, be, nu: (b, 0)),
            scratch_shapes=[pltpu.VMEM((bs, d), BF16)]),
        compiler_params=_params(2),
        name=name,
    )(block_expert, n_used, xbuf, w_gate, w_up, b_gu.reshape(N_EXPERTS, 1, 2 * D_FF),
      b_gu.reshape(N_EXPERTS, 1, 2 * D_FF), w_down, b_down.reshape(N_EXPERTS, 1, d))


def _combine_body(dest_ref, next_ref, gates_ref, h_ref, g_ref, ybuf_hbm, o_ref, buf_ref, sem, *, tc):
    i = pl.program_id(0)

    def issue_tile(idx_ref, s):
        def issue(t, carry):
            for k in range(TOP_K):
                pltpu.make_async_copy(ybuf_hbm.at[pl.ds(idx_ref[0, 0, t * TOP_K + k], 1)],
                                      buf_ref.at[s, k, pl.ds(t, 1)], sem.at[s]).start()
            return carry

        lax.fori_loop(0, tc, issue, 0, unroll=8)

    def step(slot):
        if slot == 0:
            @pl.when(i == 0)
            def _():
                issue_tile(dest_ref, 0)

        @pl.when(i + 1 < pl.num_programs(0))
        def _():
            issue_tile(next_ref, 1 - slot)

        for k in range(TOP_K):
            pltpu.make_async_copy(ybuf_hbm.at[pl.ds(0, tc)], buf_ref.at[slot, k], sem.at[slot]).wait()

        gates = gates_ref[...]
        y = h_ref[...]
        for k in range(TOP_K):
            y = y + buf_ref[slot, k] * gates[:, k:k + 1]
        o_ref[...] = _rms_scale(y, g_ref[...])

    pl.when(i % 2 == 0)(lambda: step(0))
    pl.when(i % 2 == 1)(lambda: step(1))


def combine(dest, gates, h, g_final, ybuf, tc, name):
    n, d = h.shape
    tc = min(tc, n)
    n_tiles = n // tc
    dest3 = dest.reshape(n_tiles, 1, tc * TOP_K)
    return pl.pallas_call(
        functools.partial(_combine_body, tc=tc),
        out_shape=jax.ShapeDtypeStruct((n, d), F32),
        grid=(n_tiles,),
        in_specs=[pl.BlockSpec((1, 1, tc * TOP_K), lambda i: (i, 0, 0), memory_space=pltpu.SMEM),
                  pl.BlockSpec((1, 1, tc * TOP_K), lambda i: (jnp.minimum(i + 1, n_tiles - 1), 0, 0),
                               memory_space=pltpu.SMEM),
                  pl.BlockSpec((tc, TOP_K), lambda i: (i, 0)),
                  pl.BlockSpec((tc, d), lambda i: (i, 0)),
                  pl.BlockSpec((1, d), lambda i: (0, 0)),
                  pl.BlockSpec(memory_space=pl.ANY)],
        out_specs=pl.BlockSpec((tc, d), lambda i: (i, 0)),
        scratch_shapes=[pltpu.VMEM((2, TOP_K, tc, d), F32), pltpu.SemaphoreType.DMA((2,))],
        compiler_params=_params(1),
        name=name,
    )(dest3, dest3, gates, h, g_final.reshape(1, d), ybuf)


def moe_and_final_norm(h_p, h_s, g_ffn, w_router, b_router, w_gate, w_up, b_gu, w_down, b_down, g_final, *, bs,
                       tf):
    n_p, n_s = h_p.shape[0], h_s.shape[0]
    n_assign = (n_p + n_s) * TOP_K
    n_blocks = -(-n_assign // bs) + N_EXPERTS
    no_counts = jnp.zeros((1, N_EXPERTS), I32)
    xn_p, ids_p, gates_p, rank_p, counts_p = router(h_p, g_ffn, w_router, b_router, no_counts, 512, "router_p")
    xn_s, ids_s, gates_s, rank_s, counts = router(h_s, g_ffn, w_router, b_router, counts_p, 256, "router_s")
    counts = counts.reshape(N_EXPERTS)
    padded = (counts + bs - 1) // bs * bs
    pend = jnp.cumsum(padded)
    pstart = pend - padded
    n_used = (pend[-1:] // bs).astype(I32)
    block_row0 = jnp.arange(n_blocks, dtype=I32) * bs
    block_expert = jnp.minimum(jnp.sum((pend[None, :] <= block_row0[:, None]).astype(I32), axis=1), N_EXPERTS - 1)
    dest = assignment_dest(pstart.astype(I32), jnp.concatenate([ids_p, ids_s]), jnp.concatenate([rank_p, rank_s]),
                           "dest")
    dest_p, dest_s = dest[:n_p * TOP_K], dest[n_p * TOP_K:]
    xbuf, w_down_b = dispatch(xn_p, xn_s, dest, (pstart + counts).astype(I32), (padded - counts).astype(I32),
                              n_blocks * bs, min(256, n_s), w_down.reshape(N_EXPERTS * D_FF, -1), "dispatch")
    ybuf = expert_ffn(block_expert, n_used, xbuf, w_gate, w_up, b_gu, w_down_b.reshape(w_down.shape), b_down, bs,
                      tf, "experts")
    y_p = combine(dest_p, gates_p, h_p, g_final, ybuf, 256, "combine_p")
    y_s = combine(dest_s, gates_s, h_s, g_final, ybuf, 128, "combine_s")
    return y_p, y_s


def _bias_rows(table):
    flipped = table[:, ::-1]
    left = BIAS_SPAN - 3 * REL_MAX
    right = BIAS_SPAN - left - (2 * REL_MAX + 1)
    return jnp.pad(flipped, ((0, 0), (left, right)), mode="edge").astype(F32)


def kernel(x_prompt, x_sample, cache_attn_k, cache_attn_v, state_pool, cache_mem_k, cache_mem_v, mem_prompt, g_mix,
           w_in, rel_table, pool_w, pool_scale, w_out, g_memkv, w_mk, w_mv, g_mem, w_mq, w_mo, g_ffn, w_router,
           b_router, w_gu, b_gu, w_down, b_down, g_final):
    bp, s, d = x_prompt.shape
    bsz, t, _ = x_sample.shape
    keep = min(ATT_WINDOW, s)

    def layer(a):
        assert a.shape[0] == 1
        return a.reshape(a.shape[1:])

    (g_mix, w_in, rel_table, pool_w, pool_scale, w_out, g_memkv, w_mk, w_mv, g_mem, w_mq, w_mo, g_ffn, w_router,
     b_router, w_gu, b_gu, w_down, b_down, cache_attn_k, cache_attn_v, state_pool, cache_mem_k, cache_mem_v) = map(
        layer, (g_mix, w_in, rel_table, pool_w, pool_scale, w_out, g_memkv, w_mk, w_mv, g_mem, w_mq, w_mo, g_ffn,
                w_router, b_router, w_gu, b_gu, w_down, b_down, cache_attn_k, cache_attn_v, state_pool, cache_mem_k,
                cache_mem_v))
    w_in_b = w_in.astype(BF16)
    w_out_b = w_out.astype(BF16)
    w_mk_b, w_mv_b = w_mk.astype(BF16), w_mv.astype(BF16)
    w_mq_b, w_mo_b = w_mq.astype(BF16), w_mo.astype(BF16)
    pool_w_b = pool_w.astype(BF16)
    bias_rows = _bias_rows(rel_table) * LOG2E
    w_gu_rows = w_gu.reshape(N_EXPERTS * d, 2 * D_FF)
    cast_gate, cast_up = CastJob(w_gu_rows, 0, D_FF), CastJob(w_gu_rows, 1, D_FF)

    att_scale, mem_scale = HEAD_DIM_A ** -0.5 * LOG2E, HEAD_DIM_MEM ** -0.5
    z = norm_matmul(x_prompt.reshape(bp * s, d), g_mix, w_in_b, BF16, 1024, 1024, "in_proj_p",
                    scale=att_scale, scaled_cols=A_WIDTH)
    z = z.reshape(bp, s, IN_WIDTH)
    att, w_gate_b = band_attention_prompt(z, bias_rows, 256, cast_gate)
    tm_o = 512
    h, w_up_b = outproj(att, z, 3, z, lambda bi, i, j: (bi, jnp.maximum(i * (tm_o // HIST_ROWS) - 1, 0), 3),
                        x_prompt, pool_w_b, pool_scale, w_out_b, 0, tm_o, d, True, "outproj_p", cast_up)
    mem2d = mem_prompt.reshape(bp * N_MEM, d)
    mk = norm_matmul(mem2d, g_memkv, w_mk_b, F32, 1024, 1024, "mem_k_p").reshape(bp, N_MEM, d)
    mv = norm_matmul(mem2d, g_memkv, w_mv_b, F32, 1024, 1024, "mem_v_p").reshape(bp, N_MEM, d)
    qm = norm_matmul(h.reshape(bp * s, d), g_mem, w_mq_b, BF16, 1024, 1024, "mem_q_p", scale=mem_scale,
                     scaled_cols=d).reshape(bp, s, d)
    h = mem_attention(qm, mk.astype(BF16), mv.astype(BF16), h, w_mo_b, 512, d, "mem_attn_p")
    w_gate_b = w_gate_b.reshape(N_EXPERTS, d, D_FF)
    w_up_b = w_up_b.reshape(N_EXPERTS, d, D_FF)

    new_k_p = z[:, s - keep:, A_WIDTH:2 * A_WIDTH].astype(F32).reshape(1, bp, keep, N_HEADS_A, HEAD_DIM_A)
    new_v_p = z[:, s - keep:, 2 * A_WIDTH:3 * A_WIDTH].astype(F32).reshape(1, bp, keep, N_HEADS_A, HEAD_DIM_A)
    new_pool_p = z[:, s - POOL_HIST:, 3 * A_WIDTH:].astype(F32)[None]
    new_mk_p = mk.reshape(1, bp, N_MEM, N_HEADS_MEM, HEAD_DIM_MEM)
    new_mv_p = mv.reshape(1, bp, N_MEM, N_HEADS_MEM, HEAD_DIM_MEM)

    zs = norm_matmul(x_sample.reshape(bsz * t, d), g_mix, w_in_b, BF16, 256, 1024, "in_proj_s", scale=att_scale,
                     scaled_cols=A_WIDTH)
    zs = zs.reshape(bsz, t, IN_WIDTH)
    att_s = band_attention_sample(zs, cache_attn_k, cache_attn_v, bias_rows)
    hist_s = jnp.pad(state_pool, ((0, 0), (HIST_ROWS - POOL_HIST, 0), (0, 0)))
    hs = outproj(att_s, zs, 3, hist_s, lambda bi, i, j: (bi, 0, 0), x_sample, pool_w_b, pool_scale, w_out_b,
                 PAST_LEN, t, 1024, False, "outproj_s")
    qs = norm_matmul(hs.reshape(bsz * t, d), g_mem, w_mq_b, BF16, 256, 1024, "mem_q_s", scale=mem_scale,
                     scaled_cols=d).reshape(bsz, t, d)
    hs = mem_attention(qs, cache_mem_k, cache_mem_v, hs, w_mo_b, t, 1024, "mem_attn_s")

    y_prompt, y_sample = moe_and_final_norm(h.reshape(bp * s, d), hs.reshape(bsz * t, d), g_ffn, w_router, b_router,
                                            w_gate_b, w_up_b, b_gu, w_down, b_down, g_final, bs=512, tf=1024)
    y_prompt = y_prompt.reshape(bp, s, d)
    y_sample = y_sample.reshape(bsz, t, d)

    new_k_s = zs[:, :, A_WIDTH:2 * A_WIDTH].astype(F32).reshape(1, bsz, t, N_HEADS_A, HEAD_DIM_A)
    new_v_s = zs[:, :, 2 * A_WIDTH:3 * A_WIDTH].astype(F32).reshape(1, bsz, t, N_HEADS_A, HEAD_DIM_A)
    us_ext = jnp.concatenate([state_pool, zs[:, :, 3 * A_WIDTH:].astype(F32)], axis=1)
    new_pool_s = us_ext[:, -POOL_HIST:][None]

    return (y_prompt, y_sample, new_k_p, new_v_p, new_pool_p, new_mk_p, new_mv_p, new_k_s, new_v_s, new_pool_s)
```

```python
import functools
from typing import NamedTuple, Optional

import jax
import jax.numpy as jnp
from jax import lax
from jax.experimental import pallas as pl
from jax.experimental.pallas import tpu as pltpu

F32 = jnp.float32
BF16 = jnp.bfloat16
I32 = jnp.int32

D_MODEL = 2048
PAST_LEN = 4096
CHUNK = 64
N_LEFT_CHUNKS = 8
ATT_WINDOW = N_LEFT_CHUNKS * CHUNK
A_WIDTH = 1024
N_HEADS_A = 8
HEAD_DIM_A = A_WIDTH // N_HEADS_A
REL_MAX = 128
B_WIDTH = 1024
POOL_WINDOWS = (2, 4, 8, 16)
POOL_GROUP = B_WIDTH // len(POOL_WINDOWS)
POOL_HIST = max(POOL_WINDOWS) - 1
HIST_ROWS = POOL_HIST + 1
IN_WIDTH = 3 * A_WIDTH + B_WIDTH
N_MEM = 256
N_HEADS_MEM = 4
HEAD_DIM_MEM = D_MODEL // N_HEADS_MEM
N_EXPERTS = 32
TOP_K = 4
D_FF = D_MODEL
SWIGLU_LIMIT = 7.0
SWIGLU_ALPHA = 1.702
EPS = 1e-5
NEG_INF = -1e30

V7X_VMEM_LIMIT_BYTES = 56 * 1024 * 1024
LANES = 128
BIAS_SPAN = 1024
ZERO_ROWS = 128
LOG2E = 1.4426950408889634


def _params(n_grid):
    return pltpu.CompilerParams(dimension_semantics=("arbitrary",) * n_grid,
                                vmem_limit_bytes=V7X_VMEM_LIMIT_BYTES)


def _weight_spec(block_shape, index_map, resident):
    if resident:
        return pl.BlockSpec(block_shape, index_map, pipeline_mode=pl.Buffered(1))
    return pl.BlockSpec(block_shape, index_map)


def _head_cols(ref, h, width):
    if len(ref.shape) == 3:
        return ref[:, h, :]
    return ref[:, h * width:(h + 1) * width]


def _rms_scale(x, g):
    return x * lax.rsqrt(jnp.mean(x * x, axis=-1, keepdims=True) + EPS) * g


class CastJob(NamedTuple):
    src: jax.Array
    col_block: int
    cols: int


def _call_with_cast_job(body, job, grid, in_specs, out_specs, out_shape, scratch_shapes, args, name):
    n_grid = len(grid)
    if job is None:
        return pl.pallas_call(body, out_shape=out_shape, grid=grid, in_specs=in_specs, out_specs=out_specs,
                              scratch_shapes=scratch_shapes, compiler_params=_params(n_grid), name=name)(*args)
    n_steps = 1
    for g in grid:
        n_steps *= g
    rows = job.src.shape[0]
    step_rows = rows // n_steps
    assert step_rows * n_steps == rows and step_rows % 16 == 0

    def flat_step(*idx):
        flat = idx[0]
        for g, i in zip(grid[1:], idx[1:]):
            flat = flat * g + i
        return flat

    n_in, n_out = len(in_specs), len(out_specs)

    def body_with_cast(*refs):
        ins, src_ref, rest = refs[:n_in], refs[n_in], refs[n_in + 1:]
        outs, dst_ref, scratch = rest[:n_out], rest[n_out], rest[n_out + 1:]
        body(*ins, *outs, *scratch)
        dst_ref[...] = src_ref[...].astype(BF16)

    return pl.pallas_call(
        body_with_cast,
        out_shape=tuple(out_shape) + (jax.ShapeDtypeStruct((rows, job.cols), BF16),),
        grid=grid,
        in_specs=list(in_specs) + [pl.BlockSpec((step_rows, job.cols), lambda *idx: (flat_step(*idx), job.col_block))],
        out_specs=tuple(out_specs) + (pl.BlockSpec((step_rows, job.cols), lambda *idx: (flat_step(*idx), 0)),),
        scratch_shapes=scratch_shapes,
        compiler_params=_params(n_grid),
        name=name,
    )(*args, job.src)


def _norm_matmul_body(x_ref, g_ref, w_ref, o_ref, xn_ref, *, scale, n_scaled_tiles, n_col_tiles):
    def normalize():
        xn_ref[...] = _rms_scale(x_ref[...].astype(F32), g_ref[...]).astype(BF16)

    if n_col_tiles == 1:
        normalize()
    else:
        pl.when(pl.program_id(1) == 0)(normalize)

    acc = jnp.dot(xn_ref[...], w_ref[...], preferred_element_type=F32)
    if scale is not None:
        acc = acc * jnp.where(pl.program_id(1) < n_scaled_tiles, scale, 1.0)
    o_ref[...] = acc.astype(o_ref.dtype)


def norm_matmul(x, g, w, out_dtype, tm, tn, name, cast_job=None, scale=None, scaled_cols=0):
    m, d = x.shape
    n = w.shape[1]
    tm, tn = min(tm, m), min(tn, n)
    assert scaled_cols % tn == 0
    out = _call_with_cast_job(
        functools.partial(_norm_matmul_body, scale=scale, n_scaled_tiles=scaled_cols // tn, n_col_tiles=n // tn),
        cast_job,
        (m // tm, n // tn),
        [pl.BlockSpec((tm, d), lambda i, j: (i, 0)),
         pl.BlockSpec((1, d), lambda i, j: (0, 0)),
         _weight_spec((d, tn), lambda i, j: (0, j), resident=tn == n)],
        (pl.BlockSpec((tm, tn), lambda i, j: (i, j)),),
        (jax.ShapeDtypeStruct((m, n), out_dtype),),
        [pltpu.VMEM((tm, d), BF16)],
        (x, g.reshape(1, d), w), name)
    return out[0] if cast_job is None else out


def _build_band_bias(g_ref, bias_ref, tq, tk):
    row = lax.broadcasted_iota(I32, (tq, tk), 0)
    col = lax.broadcasted_iota(I32, (tq, tk), 1)
    rel_chunk = col // CHUNK - N_LEFT_CHUNKS - row // CHUNK
    in_band = (rel_chunk <= 0) & (rel_chunk >= -N_LEFT_CHUNKS)
    lo = BIAS_SPAN - ATT_WINDOW - 2 * REL_MAX
    for h in range(N_HEADS_A):
        base = jnp.broadcast_to(g_ref[h:h + 1, :], (tq, BIAS_SPAN))
        rolled = pltpu.roll(base, 0, 1, stride=1, stride_axis=0)
        bias_ref[h] = jnp.where(in_band, rolled[:, lo:lo + tk], NEG_INF)


def _band_attn_scratch(tq, tk):
    return [pltpu.VMEM((N_HEADS_A, tq, tk), F32)]


def _band_attn_body(*refs, n_kv, tq, kw, dynamic_first):
    g_ref, q_ref = refs[0], refs[1]
    k_refs = refs[2:2 + n_kv]
    v_refs = refs[2 + n_kv:2 + 2 * n_kv]
    o_ref, bias_ref = refs[2 + 2 * n_kv], refs[3 + 2 * n_kv]
    tk = sum(kw)
    first = (pl.program_id(0) == 0) & (pl.program_id(1) == 0)

    @pl.when(first)
    def _():
        _build_band_bias(g_ref, bias_ref, tq, tk)

    def all_heads(mask_missing_keys):
        if mask_missing_keys:
            col = lax.broadcasted_iota(I32, (tq, tk), 1)
            exists = col >= ATT_WINDOW - pl.program_id(1) * tq
        for h in range(N_HEADS_A):
            sl = slice(h * HEAD_DIM_A, (h + 1) * HEAD_DIM_A)
            q = q_ref[:, sl]
            s = jnp.concatenate(
                [lax.dot_general(q, _head_cols(k_ref, h, HEAD_DIM_A).astype(BF16), (((1,), (1,)), ((), ())),
                                 preferred_element_type=F32) for k_ref in k_refs], axis=1)
            s = s + bias_ref[h]
            if mask_missing_keys:
                s = jnp.where(exists, s, NEG_INF)
            p = jnp.exp2(s - jnp.max(s, axis=-1, keepdims=True))
            l = jnp.sum(p, axis=-1, keepdims=True)
            pb = p.astype(BF16)
            o = None
            off = 0
            for v_ref, w in zip(v_refs, kw):
                part = jnp.dot(pb[:, off:off + w], _head_cols(v_ref, h, HEAD_DIM_A).astype(BF16),
                               preferred_element_type=F32)
                o = part if o is None else o + part
                off += w
            o_ref[:, sl] = (o / l).astype(o_ref.dtype)

    if dynamic_first:
        some_missing = pl.program_id(1) * tq < ATT_WINDOW
        pl.when(some_missing)(lambda: all_heads(True))
        pl.when(jnp.logical_not(some_missing))(lambda: all_heads(False))
    else:
        all_heads(False)


def band_attention_prompt(z, bias_rows, tq, cast_job=None):
    b, s, _ = z.shape
    n_kv = ATT_WINDOW // tq + 1
    tk = n_kv * tq
    blk = (None, tq, A_WIDTH)

    def kv_spec(back, colblk):
        return pl.BlockSpec(blk, lambda bi, i: (bi, jnp.maximum(i - back, 0), colblk))

    in_specs = [pl.BlockSpec((N_HEADS_A, BIAS_SPAN), lambda bi, i: (0, 0)),
                pl.BlockSpec(blk, lambda bi, i: (bi, i, 0))]
    in_specs += [kv_spec(n_kv - 1 - j, 1) for j in range(n_kv)]
    in_specs += [kv_spec(n_kv - 1 - j, 2) for j in range(n_kv)]
    out = _call_with_cast_job(
        functools.partial(_band_attn_body, n_kv=n_kv, tq=tq, kw=(tq,) * n_kv, dynamic_first=True), cast_job,
        (b, s // tq), in_specs, (pl.BlockSpec(blk, lambda bi, i: (bi, i, 0)),),
        (jax.ShapeDtypeStruct((b, s, A_WIDTH), BF16),),
        _band_attn_scratch(tq, tk),
        (bias_rows,) + (z,) * (1 + 2 * n_kv), "band_attn_prompt")
    return out[0] if cast_job is None else out


def band_attention_sample(z, cache_k, cache_v, bias_rows):
    b, t, _ = z.shape
    w = cache_k.shape[1]
    assert w == ATT_WINDOW and PAST_LEN >= w and PAST_LEN % CHUNK == 0 and t <= CHUNK
    new_blk = (None, t, A_WIDTH)
    old_blk = (None, w, N_HEADS_A, HEAD_DIM_A)
    return pl.pallas_call(
        functools.partial(_band_attn_body, n_kv=2, tq=t, kw=(w, t), dynamic_first=False),
        out_shape=jax.ShapeDtypeStruct((b, t, A_WIDTH), BF16),
        grid=(b, 1),
        in_specs=[pl.BlockSpec((N_HEADS_A, BIAS_SPAN), lambda bi, i: (0, 0)),
                  pl.BlockSpec(new_blk, lambda bi, i: (bi, 0, 0)),
                  pl.BlockSpec(old_blk, lambda bi, i: (bi, 0, 0, 0)),
                  pl.BlockSpec(new_blk, lambda bi, i: (bi, 0, 1)),
                  pl.BlockSpec(old_blk, lambda bi, i: (bi, 0, 0, 0)),
                  pl.BlockSpec(new_blk, lambda bi, i: (bi, 0, 2))],
        out_specs=pl.BlockSpec(new_blk, lambda bi, i: (bi, 0, 0)),
        scratch_shapes=_band_attn_scratch(t, w + t),
        compiler_params=_params(2),
        name="band_attn_sample",
    )(bias_rows, z, cache_k, z, cache_v, z)


def _outproj_body(att_ref, u_ref, hist_ref, x_ref, wp_ref, ps_ref, w_ref, o_ref, cat_ref, *, pos0, tm,
                  zero_first_hist):
    i = pl.program_id(1)

    @pl.when(pl.program_id(2) == 0)
    def _():
        cat_ref[:, :A_WIDTH] = att_ref[...]
        hist = hist_ref[...].astype(F32)
        if zero_first_hist:
            hist = jnp.where(i > 0, hist, 0.0)
        ext = jnp.concatenate([hist, u_ref[...].astype(F32)], axis=0)
        pos = pos0 + i * tm + lax.broadcasted_iota(I32, (tm, 1), 0)
        for g, win in enumerate(POOL_WINDOWS):
            sl = slice(g * POOL_GROUP, (g + 1) * POOL_GROUP)
            e = ext[:, sl]
            run, span = e, 1
            while span < win:
                run = run[span:] + run[:-span]
                span *= 2
            first = HIST_ROWS - win + 1
            wsum = run[first:first + tm]
            cnt = jnp.minimum(pos + 1, win).astype(F32)
            d = (wsum / cnt - e[HIST_ROWS:]).astype(BF16)
            pool = jnp.dot(d, wp_ref[g], preferred_element_type=F32) * ps_ref[:, sl]
            cat_ref[:, A_WIDTH + g * POOL_GROUP:A_WIDTH + (g + 1) * POOL_GROUP] = pool.astype(BF16)

    o_ref[...] = x_ref[...] + jnp.dot(cat_ref[...], w_ref[...], preferred_element_type=F32)


def outproj(att, u_src, u_colblk, hist_src, hist_map, x, pool_w, pool_scale, w_out, pos0, tm, tn,
            zero_first_hist, name, cast_job=None):
    b, s, d = x.shape
    tm = min(tm, s)
    out = _call_with_cast_job(
        functools.partial(_outproj_body, pos0=pos0, tm=tm, zero_first_hist=zero_first_hist), cast_job,
        (b, s // tm, d // tn),
        [pl.BlockSpec((None, tm, A_WIDTH), lambda bi, i, j: (bi, i, 0)),
         pl.BlockSpec((None, tm, B_WIDTH), lambda bi, i, j: (bi, i, u_colblk)),
         pl.BlockSpec((None, HIST_ROWS, B_WIDTH), hist_map),
         pl.BlockSpec((None, tm, tn), lambda bi, i, j: (bi, i, j)),
         pl.BlockSpec((len(POOL_WINDOWS), POOL_GROUP, POOL_GROUP), lambda bi, i, j: (0, 0, 0)),
         pl.BlockSpec((1, B_WIDTH), lambda bi, i, j: (0, 0)),
         _weight_spec((A_WIDTH + B_WIDTH, tn), lambda bi, i, j: (0, j), resident=tn == d)],
        (pl.BlockSpec((None, tm, tn), lambda bi, i, j: (bi, i, j)),),
        (jax.ShapeDtypeStruct((b, s, d), F32),),
        [pltpu.VMEM((tm, A_WIDTH + B_WIDTH), BF16)],
        (att, u_src, hist_src, x, pool_w, pool_scale.reshape(1, B_WIDTH), w_out), name)
    return out[0] if cast_job is None else out


def _memattn_body(q_ref, mk_ref, mv_ref, x_ref, w_ref, o_ref, oh_ref):
    @pl.when(pl.program_id(2) == 0)
    def _():
        for h in range(N_HEADS_MEM):
            sl = slice(h * HEAD_DIM_MEM, (h + 1) * HEAD_DIM_MEM)
            s = lax.dot_general(q_ref[:, sl], _head_cols(mk_ref, h, HEAD_DIM_MEM).astype(BF16),
                                (((1,), (1,)), ((), ())), preferred_element_type=F32)
            p = jnp.exp(s - jnp.max(s, axis=-1, keepdims=True))
            l = jnp.sum(p, axis=-1, keepdims=True)
            o = jnp.dot(p.astype(BF16), _head_cols(mv_ref, h, HEAD_DIM_MEM).astype(BF16),
                        preferred_element_type=F32) / l
            oh_ref[:, sl] = o.astype(BF16)

    o_ref[...] = x_ref[...] + jnp.dot(oh_ref[...], w_ref[...], preferred_element_type=F32)


def mem_attention(q, mk, mv, x, w_mo, tm, tn, name, cast_job=None):
    b, s, d = x.shape
    tm = min(tm, s)
    kv_spec = pl.BlockSpec((None,) + mk.shape[1:], lambda bi, i, j: (bi,) + (0,) * (mk.ndim - 1))
    out = _call_with_cast_job(
        _memattn_body, cast_job, (b, s // tm, d // tn),
        [pl.BlockSpec((None, tm, d), lambda bi, i, j: (bi, i, 0)),
         kv_spec,
         kv_spec,
         pl.BlockSpec((None, tm, tn), lambda bi, i, j: (bi, i, j)),
         _weight_spec((d, tn), lambda bi, i, j: (0, j), resident=tn == d)],
        (pl.BlockSpec((None, tm, tn), lambda bi, i, j: (bi, i, j)),),
        (jax.ShapeDtypeStruct((b, s, d), F32),),
        [pltpu.VMEM((tm, d), BF16)],
        (q, mk, mv, x, w_mo), name)
    return out[0] if cast_job is None else out


def _router_body(x_ref, g_ref, w_ref, b_ref, cin_ref, xn_ref, ids_ref, gates_ref, rank_ref, cnt_ref, carry_ref, *,
                 tm):
    @pl.when(pl.program_id(0) == 0)
    def _():
        carry_ref[...] = cin_ref[...].astype(F32)

    xn = _rms_scale(x_ref[...], g_ref[...])
    xn_ref[...] = xn
    x_hi = xn.astype(BF16)
    x_lo = (xn - x_hi.astype(F32)).astype(BF16)
    w = w_ref[...]
    w_hi = w.astype(BF16)
    w_lo = (w - w_hi.astype(F32)).astype(BF16)
    prod = jnp.dot(jnp.concatenate([x_hi, x_lo], axis=0), jnp.concatenate([w_hi, w_lo], axis=1),
                   preferred_element_type=F32)
    logits = (prod[:tm, :N_EXPERTS] + prod[tm:, :N_EXPERTS] + prod[:tm, N_EXPERTS:]) + b_ref[...]

    lane = lax.broadcasted_iota(I32, (tm, N_EXPERTS), 1)
    kcol = lax.broadcasted_iota(I32, (tm, TOP_K), 1)
    work = logits
    multi_hot = jnp.zeros((tm, N_EXPERTS), F32)
    ids = jnp.zeros((tm, TOP_K), I32)
    tops = jnp.zeros((tm, TOP_K), F32)
    picks = []
    for k in range(TOP_K):
        best = jnp.max(work, axis=-1, keepdims=True)
        idx = jnp.min(jnp.where(work == best, lane, N_EXPERTS), axis=-1, keepdims=True)
        hit = lane == idx
        picks.append(hit)
        multi_hot = multi_hot + hit.astype(F32)
        ids = jnp.where(kcol == k, idx, ids)
        tops = jnp.where(kcol == k, best, tops)
        work = jnp.where(hit, -jnp.inf, work)
    e = jnp.exp(tops - tops[:, 0:1])
    gates_ref[...] = e / jnp.sum(e, axis=-1, keepdims=True)
    ids_ref[...] = ids

    r = lax.broadcasted_iota(I32, (tm, tm), 0)
    c = lax.broadcasted_iota(I32, (tm, tm), 1)
    earlier = (c < r).astype(BF16)
    before = jnp.dot(earlier, multi_hot.astype(BF16), preferred_element_type=F32) + carry_ref[...]
    rank = jnp.zeros((tm, TOP_K), F32)
    for k in range(TOP_K):
        rk = jnp.sum(jnp.where(picks[k], before, 0.0), axis=-1, keepdims=True)
        rank = jnp.where(kcol == k, rk, rank)
    rank_ref[...] = rank.astype(I32)
    total = carry_ref[...] + jnp.sum(multi_hot, axis=0, keepdims=True)
    carry_ref[...] = total
    cnt_ref[...] = total.astype(I32)


def router(x, g, w_router, b_router, counts_in, tm, name):
    n, d = x.shape
    tm = min(tm, n)
    row = lambda i: (i, 0)
    fixed = lambda i: (0, 0)
    return pl.pallas_call(
        functools.partial(_router_body, tm=tm),
        out_shape=(jax.ShapeDtypeStruct((n, d), F32),
                   jax.ShapeDtypeStruct((n, TOP_K), I32),
                   jax.ShapeDtypeStruct((n, TOP_K), F32),
                   jax.ShapeDtypeStruct((n, TOP_K), I32),
                   jax.ShapeDtypeStruct((1, N_EXPERTS), I32)),
        grid=(n // tm,),
        in_specs=[pl.BlockSpec((tm, d), row), pl.BlockSpec((1, d), fixed),
                  pl.BlockSpec((d, N_EXPERTS), fixed), pl.BlockSpec((1, N_EXPERTS), fixed),
                  pl.BlockSpec((1, N_EXPERTS), fixed)],
        out_specs=(pl.BlockSpec((tm, d), row), pl.BlockSpec((tm, TOP_K), row), pl.BlockSpec((tm, TOP_K), row),
                   pl.BlockSpec((tm, TOP_K), row), pl.BlockSpec((1, N_EXPERTS), fixed)),
        scratch_shapes=[pltpu.VMEM((1, N_EXPERTS), F32)],
        compiler_params=_params(1),
        name=name,
    )(x, g.reshape(1, d), w_router, b_router.reshape(1, N_EXPERTS), counts_in)


def _dest_body(pstart_ref, ids_ref, rank_ref, dest_ref):
    ids = ids_ref[...]
    dest = rank_ref[...]
    for e in range(N_EXPERTS):
        dest = dest + jnp.where(ids == e, pstart_ref[e], 0)
    dest_ref[...] = dest


def assignment_dest(pstart, ids, rank, name):
    n_assign = ids.size
    shape = (n_assign // LANES, LANES)
    spec = pl.BlockSpec(shape, lambda i, ps: (0, 0))
    dest = pl.pallas_call(
        _dest_body,
        out_shape=jax.ShapeDtypeStruct(shape, I32),
        grid_spec=pltpu.PrefetchScalarGridSpec(num_scalar_prefetch=1, grid=(1,), in_specs=[spec, spec],
                                               out_specs=spec),
        compiler_params=_params(1),
        name=name,
    )(pstart, ids.reshape(shape), rank.reshape(shape))
    return dest.reshape(n_assign)


def _dispatch_body(fill_ref, cnt_ref, dest_ref, xp_ref, xs_ref, wsrc_ref, xbuf_hbm, wdst_ref, zero_ref, sem, zsem, *,
                   n_p_tiles):
    wdst_ref[...] = wsrc_ref[...].astype(BF16)

    def scatter_tile(x_ref):
        n_tok = x_ref.shape[0]

        def issue(t, carry):
            for k in range(TOP_K):
                pltpu.make_async_copy(x_ref.at[pl.ds(t, 1)],
                                      xbuf_hbm.at[pl.ds(dest_ref[0, 0, t * TOP_K + k], 1)], sem).start()
            return carry

        lax.fori_loop(0, n_tok, issue, 0, unroll=8)
        for k in range(TOP_K):
            pltpu.make_async_copy(x_ref, xbuf_hbm.at[pl.ds(0, n_tok)], sem).wait()

    @pl.when(pl.program_id(0) < n_p_tiles)
    def _():
        scatter_tile(xp_ref)

    @pl.when(pl.program_id(0) >= n_p_tiles)
    def _():
        scatter_tile(xs_ref)

    @pl.when(pl.program_id(0) == 0)
    def _():
        zero_ref[...] = jnp.zeros_like(zero_ref)

        def pad_copy(dst):
            return pltpu.make_async_copy(zero_ref.at[pl.ds(0, 1)], xbuf_hbm.at[pl.ds(dst, 1)], zsem)

        def tail_copy(c):
            return pltpu.make_async_copy(zero_ref, xbuf_hbm.at[pl.ds(pl.multiple_of(c * ZERO_ROWS, ZERO_ROWS),
                                                                     ZERO_ROWS)], zsem)

        first_tail = (fill_ref[N_EXPERTS - 1] + cnt_ref[N_EXPERTS - 1]) // ZERO_ROWS
        n_tail = xbuf_hbm.shape[0] // ZERO_ROWS

        def tail_start(c, carry):
            tail_copy(c).start()
            return carry

        def tail_wait(c, carry):
            tail_copy(c).wait()
            return carry

        lax.fori_loop(first_tail, n_tail, tail_start, 0)
        lax.fori_loop(first_tail, n_tail, tail_wait, 0)

        def per_expert(e, carry):
            start = fill_ref[e]
            n_pad = cnt_ref[e]

            def one(r, c2):
                pad_copy(start + r).start()
                return c2

            lax.fori_loop(0, n_pad, one, 0)

            def one_wait(r, c2):
                pad_copy(start + r).wait()
                return c2

            lax.fori_loop(0, n_pad, one_wait, 0)
            return carry

        lax.fori_loop(0, N_EXPERTS, per_expert, 0)


def dispatch(xn_p, xn_s, dest_p, dest_s, pad_start, pad_count, n_rows, tc, w_rows, name):
    (n_p, d), n_s = xn_p.shape, xn_s.shape[0]
    n_p_tiles = n_p // tc
    assert n_p_tiles * tc == n_p and n_s <= tc and n_s % 8 == 0
    dest = jnp.concatenate([dest_p, jnp.pad(dest_s, (0, (tc - n_s) * TOP_K))]).reshape(n_p_tiles + 1, 1, tc * TOP_K)
    w_step_rows = w_rows.shape[0] // n_p_tiles
    assert w_step_rows * n_p_tiles == w_rows.shape[0] and w_step_rows % 16 == 0
    w_spec = pl.BlockSpec((w_step_rows, w_rows.shape[1]), lambda i, a, b: (jnp.minimum(i, n_p_tiles - 1), 0))
    return pl.pallas_call(
        functools.partial(_dispatch_body, n_p_tiles=n_p_tiles),
        out_shape=(jax.ShapeDtypeStruct((n_rows, d), xn_p.dtype), jax.ShapeDtypeStruct(w_rows.shape, BF16)),
        grid_spec=pltpu.PrefetchScalarGridSpec(
            num_scalar_prefetch=2,
            grid=(n_p_tiles + 1,),
            in_specs=[pl.BlockSpec((1, 1, tc * TOP_K), lambda i, a, b: (i, 0, 0), memory_space=pltpu.SMEM),
                      pl.BlockSpec((tc, d), lambda i, a, b: (jnp.minimum(i, n_p_tiles - 1), 0)),
                      pl.BlockSpec((n_s, d), lambda i, a, b: (0, 0)),
                      w_spec],
            out_specs=(pl.BlockSpec(memory_space=pl.ANY), w_spec),
            scratch_shapes=[pltpu.VMEM((ZERO_ROWS, d), xn_p.dtype), pltpu.SemaphoreType.DMA,
                            pltpu.SemaphoreType.DMA]),
        compiler_params=_params(1),
        name=name,
    )(pad_start, pad_count, dest, xn_p, xn_s, w_rows)


def _expert_body(be_ref, nu_ref, x_ref, wg_ref, wu_ref, bg_ref, bu_ref, wd_ref, bd_ref, o_ref, xb_ref):
    j = pl.program_id(1)

    @pl.when(pl.program_id(0) < nu_ref[0])
    def _():
        @pl.when(j == 0)
        def _():
            xb_ref[...] = x_ref[...].astype(BF16)
            o_ref[...] = jnp.broadcast_to(bd_ref[...], o_ref.shape)

        xb = xb_ref[...]
        gate = jnp.dot(xb, wg_ref[...], preferred_element_type=F32) + bg_ref[...]
        up = jnp.dot(xb, wu_ref[...], preferred_element_type=F32) + bu_ref[...]
        gate = jnp.minimum(gate, SWIGLU_LIMIT)
        up = jnp.clip(up, -SWIGLU_LIMIT, SWIGLU_LIMIT)
        act = (up + 1.0) * (gate * jax.nn.sigmoid(SWIGLU_ALPHA * gate))
        o_ref[...] += jnp.dot(act.astype(BF16), wd_ref[...], preferred_element_type=F32)

    @pl.when((pl.program_id(0) >= nu_ref[0]) & (j == 0))
    def _():
        o_ref[...] = jnp.zeros_like(o_ref)


def expert_ffn(block_expert, n_used, xbuf, w_gate, w_up, b_gu, w_down, b_down, bs, tf, name):
    n_rows, d = xbuf.shape
    n_blocks = n_rows // bs
    nt = D_FF // tf

    def blk(b, nu):
        return jnp.minimum(b, nu[0] - 1)

    def ff(b, j, nu):
        return jnp.where(b < nu[0], j, nt - 1)

    return pl.pallas_call(
        _expert_body,
        out_shape=jax.ShapeDtypeStruct((n_rows, d), F32),
        grid_spec=pltpu.PrefetchScalarGridSpec(
            num_scalar_prefetch=2,
            grid=(n_blocks, nt),
            in_specs=[pl.BlockSpec((bs, d), lambda b, j, be, nu: (blk(b, nu), 0)),
                      pl.BlockSpec((None, d, tf), lambda b, j, be, nu: (be[blk(b, nu)], 0, ff(b, j, nu))),
                      pl.BlockSpec((None, d, tf), lambda b, j, be, nu: (be[blk(b, nu)], 0, ff(b, j, nu))),
                      pl.BlockSpec((None, 1, tf), lambda b, j, be, nu: (be[blk(b, nu)], 0, ff(b, j, nu))),
                      pl.BlockSpec((None, 1, tf), lambda b, j, be, nu: (be[blk(b, nu)], 0, nt + ff(b, j, nu))),
                      pl.BlockSpec((None, tf, d), lambda b, j, be, nu: (be[blk(b, nu)], ff(b, j, nu), 0)),
                      pl.BlockSpec((None, 1, d), lambda b, j, be, nu: (be[blk(b, nu)], 0, 0))],
            out_specs=pl.BlockSpec((bs, d), lambda b, j, be, nu: (b, 0)),
            scratch_shapes=[pltpu.VMEM((bs, d), BF16)]),
        compiler_params=_params(2),
        name=name,
    )(block_expert, n_used, xbuf, w_gate, w_up, b_gu.reshape(N_EXPERTS, 1, 2 * D_FF),
      b_gu.reshape(N_EXPERTS, 1, 2 * D_FF), w_down, b_down.reshape(N_EXPERTS, 1, d))


def _combine_body(dest_ref, next_ref, gates_ref, h_ref, g_ref, ybuf_hbm, o_ref, buf_ref, sem, *, tc):
    i = pl.program_id(0)

    def issue_tile(idx_ref, s):
        def issue(t, carry):
            for k in range(TOP_K):
                pltpu.make_async_copy(ybuf_hbm.at[pl.ds(idx_ref[0, 0, t * TOP_K + k], 1)],
                                      buf_ref.at[s, k, pl.ds(t, 1)], sem.at[s]).start()
            return carry

        lax.fori_loop(0, tc, issue, 0, unroll=8)

    def step(slot):
        if slot == 0:
            @pl.when(i == 0)
            def _():
                issue_tile(dest_ref, 0)

        @pl.when(i + 1 < pl.num_programs(0))
        def _():
            issue_tile(next_ref, 1 - slot)

        for k in range(TOP_K):
            pltpu.make_async_copy(ybuf_hbm.at[pl.ds(0, tc)], buf_ref.at[slot, k], sem.at[slot]).wait()

        gates = gates_ref[...]
        y = h_ref[...]
        for k in range(TOP_K):
            y = y + buf_ref[slot, k] * gates[:, k:k + 1]
        o_ref[...] = _rms_scale(y, g_ref[...])

    pl.when(i % 2 == 0)(lambda: step(0))
    pl.when(i % 2 == 1)(lambda: step(1))


def combine(dest, gates, h, g_final, ybuf, tc, name):
    n, d = h.shape
    tc = min(tc, n)
    n_tiles = n // tc
    dest3 = dest.reshape(n_tiles, 1, tc * TOP_K)
    return pl.pallas_call(
        functools.partial(_combine_body, tc=tc),
        out_shape=jax.ShapeDtypeStruct((n, d), F32),
        grid=(n_tiles,),
        in_specs=[pl.BlockSpec((1, 1, tc * TOP_K), lambda i: (i, 0, 0), memory_space=pltpu.SMEM),
                  pl.BlockSpec((1, 1, tc * TOP_K), lambda i: (jnp.minimum(i + 1, n_tiles - 1), 0, 0),
                               memory_space=pltpu.SMEM),
                  pl.BlockSpec((tc, TOP_K), lambda i: (i, 0)),
                  pl.BlockSpec((tc, d), lambda i: (i, 0)),
                  pl.BlockSpec((1, d), lambda i: (0, 0)),
                  pl.BlockSpec(memory_space=pl.ANY)],
        out_specs=pl.BlockSpec((tc, d), lambda i: (i, 0)),
        scratch_shapes=[pltpu.VMEM((2, TOP_K, tc, d), F32), pltpu.SemaphoreType.DMA((2,))],
        compiler_params=_params(1),
        name=name,
    )(dest3, dest3, gates, h, g_final.reshape(1, d), ybuf)


def moe_and_final_norm(h_p, h_s, g_ffn, w_router, b_router, w_gate, w_up, b_gu, w_down, b_down, g_final, *, bs,
                       tf):
    n_p, n_s = h_p.shape[0], h_s.shape[0]
    n_assign = (n_p + n_s) * TOP_K
    n_blocks = -(-n_assign // bs) + N_EXPERTS
    no_counts = jnp.zeros((1, N_EXPERTS), I32)
    xn_p, ids_p, gates_p, rank_p, counts_p = router(h_p, g_ffn, w_router, b_router, no_counts, 512, "router_p")
    xn_s, ids_s, gates_s, rank_s, counts = router(h_s, g_ffn, w_router, b_router, counts_p, 256, "router_s")
    counts = counts.reshape(N_EXPERTS)
    padded = (counts + bs - 1) // bs * bs
    pend = jnp.cumsum(padded)
    pstart = pend - padded
    n_used = (pend[-1:] // bs).astype(I32)
    block_row0 = jnp.arange(n_blocks, dtype=I32) * bs
    block_expert = jnp.minimum(jnp.sum((pend[None, :] <= block_row0[:, None]).astype(I32), axis=1), N_EXPERTS - 1)
    dest = assignment_dest(pstart.astype(I32), jnp.concatenate([ids_p, ids_s]), jnp.concatenate([rank_p, rank_s]),
                           "dest")
    dest_p, dest_s = dest[:n_p * TOP_K], dest[n_p * TOP_K:]
    xbuf, w_down_b = dispatch(xn_p, xn_s, dest_p, dest_s, (pstart + counts).astype(I32),
                              (padded - counts).astype(I32), n_blocks * bs, min(512, n_p),
                              w_down.reshape(N_EXPERTS * D_FF, -1), "dispatch")
    ybuf = expert_ffn(block_expert, n_used, xbuf, w_gate, w_up, b_gu, w_down_b.reshape(w_down.shape), b_down, bs,
                      tf, "experts")
    y_p = combine(dest_p, gates_p, h_p, g_final, ybuf, 256, "combine_p")
    y_s = combine(dest_s, gates_s, h_s, g_final, ybuf, 128, "combine_s")
    return y_p, y_s


def _bias_rows(table):
    flipped = table[:, ::-1]
    left = BIAS_SPAN - 3 * REL_MAX
    right = BIAS_SPAN - left - (2 * REL_MAX + 1)
    return jnp.pad(flipped, ((0, 0), (left, right)), mode="edge").astype(F32)


def kernel(x_prompt, x_sample, cache_attn_k, cache_attn_v, state_pool, cache_mem_k, cache_mem_v, mem_prompt, g_mix,
           w_in, rel_table, pool_w, pool_scale, w_out, g_memkv, w_mk, w_mv, g_mem, w_mq, w_mo, g_ffn, w_router,
           b_router, w_gu, b_gu, w_down, b_down, g_final):
    bp, s, d = x_prompt.shape
    bsz, t, _ = x_sample.shape
    keep = min(ATT_WINDOW, s)

    def layer(a):
        assert a.shape[0] == 1
        return a.reshape(a.shape[1:])

    (g_mix, w_in, rel_table, pool_w, pool_scale, w_out, g_memkv, w_mk, w_mv, g_mem, w_mq, w_mo, g_ffn, w_router,
     b_router, w_gu, b_gu, w_down, b_down, cache_attn_k, cache_attn_v, state_pool, cache_mem_k, cache_mem_v) = map(
        layer, (g_mix, w_in, rel_table, pool_w, pool_scale, w_out, g_memkv, w_mk, w_mv, g_mem, w_mq, w_mo, g_ffn,
                w_router, b_router, w_gu, b_gu, w_down, b_down, cache_attn_k, cache_attn_v, state_pool, cache_mem_k,
                cache_mem_v))
    w_in_b = w_in.astype(BF16)
    w_out_b = w_out.astype(BF16)
    w_mk_b, w_mv_b = w_mk.astype(BF16), w_mv.astype(BF16)
    w_mq_b, w_mo_b = w_mq.astype(BF16), w_mo.astype(BF16)
    pool_w_b = pool_w.astype(BF16)
    bias_rows = _bias_rows(rel_table) * LOG2E
    w_gu_rows = w_gu.reshape(N_EXPERTS * d, 2 * D_FF)
    cast_gate, cast_up = CastJob(w_gu_rows, 0, D_FF), CastJob(w_gu_rows, 1, D_FF)

    att_scale, mem_scale = HEAD_DIM_A ** -0.5 * LOG2E, HEAD_DIM_MEM ** -0.5
    z = norm_matmul(x_prompt.reshape(bp * s, d), g_mix, w_in_b, BF16, 1024, 1024, "in_proj_p",
                    scale=att_scale, scaled_cols=A_WIDTH)
    z = z.reshape(bp, s, IN_WIDTH)
    att, w_gate_b = band_attention_prompt(z, bias_rows, 256, cast_gate)
    tm_o = 512
    h, w_up_b = outproj(att, z, 3, z, lambda bi, i, j: (bi, jnp.maximum(i * (tm_o // HIST_ROWS) - 1, 0), 3),
                        x_prompt, pool_w_b, pool_scale, w_out_b, 0, tm_o, d, True, "outproj_p", cast_up)
    mem2d = mem_prompt.reshape(bp * N_MEM, d)
    mk = norm_matmul(mem2d, g_memkv, w_mk_b, F32, 1024, 1024, "mem_k_p").reshape(bp, N_MEM, d)
    mv = norm_matmul(mem2d, g_memkv, w_mv_b, F32, 1024, 1024, "mem_v_p").reshape(bp, N_MEM, d)
    qm = norm_matmul(h.reshape(bp * s, d), g_mem, w_mq_b, BF16, 1024, d, "mem_q_p", scale=mem_scale,
                     scaled_cols=d).reshape(bp, s, d)
    h = mem_attention(qm, mk.astype(BF16), mv.astype(BF16), h, w_mo_b, 512, d, "mem_attn_p")
    w_gate_b = w_gate_b.reshape(N_EXPERTS, d, D_FF)
    w_up_b = w_up_b.reshape(N_EXPERTS, d, D_FF)

    new_k_p = z[:, s - keep:, A_WIDTH:2 * A_WIDTH].astype(F32).reshape(1, bp, keep, N_HEADS_A, HEAD_DIM_A)
    new_v_p = z[:, s - keep:, 2 * A_WIDTH:3 * A_WIDTH].astype(F32).reshape(1, bp, keep, N_HEADS_A, HEAD_DIM_A)
    new_pool_p = z[:, s - POOL_HIST:, 3 * A_WIDTH:].astype(F32)[None]
    new_mk_p = mk.reshape(1, bp, N_MEM, N_HEADS_MEM, HEAD_DIM_MEM)
    new_mv_p = mv.reshape(1, bp, N_MEM, N_HEADS_MEM, HEAD_DIM_MEM)

    zs = norm_matmul(x_sample.reshape(bsz * t, d), g_mix, w_in_b, BF16, 256, 1024, "in_proj_s", scale=att_scale,
                     scaled_cols=A_WIDTH)
    zs = zs.reshape(bsz, t, IN_WIDTH)
    att_s = band_attention_sample(zs, cache_attn_k, cache_attn_v, bias_rows)
    hist_s = jnp.pad(state_pool, ((0, 0), (HIST_ROWS - POOL_HIST, 0), (0, 0)))
    hs = outproj(att_s, zs, 3, hist_s, lambda bi, i, j: (bi, 0, 0), x_sample, pool_w_b, pool_scale, w_out_b,
                 PAST_LEN, t, 1024, False, "outproj_s")
    qs = norm_matmul(hs.reshape(bsz * t, d), g_mem, w_mq_b, BF16, 256, 1024, "mem_q_s", scale=mem_scale,
                     scaled_cols=d).reshape(bsz, t, d)
    hs = mem_attention(qs, cache_mem_k, cache_mem_v, hs, w_mo_b, t, 1024, "mem_attn_s")

    y_prompt, y_sample = moe_and_final_norm(h.reshape(bp * s, d), hs.reshape(bsz * t, d), g_ffn, w_router, b_router,
                                            w_gate_b, w_up_b, b_gu, w_down, b_down, g_final, bs=512, tf=1024)
    y_prompt = y_prompt.reshape(bp, s, d)
    y_sample = y_sample.reshape(bsz, t, d)

    new_k_s = zs[:, :, A_WIDTH:2 * A_WIDTH].astype(F32).reshape(1, bsz, t, N_HEADS_A, HEAD_DIM_A)
    new_v_s = zs[:, :, 2 * A_WIDTH:3 * A_WIDTH].astype(F32).reshape(1, bsz, t, N_HEADS_A, HEAD_DIM_A)
    us_ext = jnp.concatenate([state_pool, zs[:, :, 3 * A_WIDTH:].astype(F32)], axis=1)
    new_pool_s = us_ext[:, -POOL_HIST:][None]

    return (y_prompt, y_sample, new_k_p, new_v_p, new_pool_p, new_mk_p, new_mv_p, new_k_s, new_v_s, new_pool_s)
```

```python
import functools
from typing import NamedTuple, Optional

import jax
import jax.numpy as jnp
from jax import lax
from jax.experimental import pallas as pl
from jax.experimental.pallas import tpu as pltpu

F32 = jnp.float32
BF16 = jnp.bfloat16
I32 = jnp.int32

D_MODEL = 2048
PAST_LEN = 4096
CHUNK = 64
N_LEFT_CHUNKS = 8
ATT_WINDOW = N_LEFT_CHUNKS * CHUNK
A_WIDTH = 1024
N_HEADS_A = 8
HEAD_DIM_A = A_WIDTH // N_HEADS_A
REL_MAX = 128
B_WIDTH = 1024
POOL_WINDOWS = (2, 4, 8, 16)
POOL_GROUP = B_WIDTH // len(POOL_WINDOWS)
POOL_HIST = max(POOL_WINDOWS) - 1
HIST_ROWS = POOL_HIST + 1
IN_WIDTH = 3 * A_WIDTH + B_WIDTH
N_MEM = 256
N_HEADS_MEM = 4
HEAD_DIM_MEM = D_MODEL // N_HEADS_MEM
N_EXPERTS = 32
TOP_K = 4
D_FF = D_MODEL
SWIGLU_LIMIT = 7.0
SWIGLU_ALPHA = 1.702
EPS = 1e-5
NEG_INF = -1e30

V7X_VMEM_LIMIT_BYTES = 56 * 1024 * 1024
LANES = 128
BIAS_SPAN = 1024
ZERO_ROWS = 128
LOG2E = 1.4426950408889634


def _params(n_grid):
    return pltpu.CompilerParams(dimension_semantics=("arbitrary",) * n_grid,
                                vmem_limit_bytes=V7X_VMEM_LIMIT_BYTES)


def _weight_spec(block_shape, index_map, resident):
    if resident:
        return pl.BlockSpec(block_shape, index_map, pipeline_mode=pl.Buffered(1))
    return pl.BlockSpec(block_shape, index_map)


def _once_per_row_tile(fn, n_col_tiles, col_axis):
    if n_col_tiles == 1:
        fn()
    else:
        pl.when(pl.program_id(col_axis) == 0)(fn)


def _head_cols(ref, h, width):
    if len(ref.shape) == 3:
        return ref[:, h, :]
    return ref[:, h * width:(h + 1) * width]


def _rms_scale(x, g):
    return x * lax.rsqrt(jnp.mean(x * x, axis=-1, keepdims=True) + EPS) * g


class CastJob(NamedTuple):
    src: jax.Array
    col_block: int
    cols: int


def _call_with_cast_job(body, job, grid, in_specs, out_specs, out_shape, scratch_shapes, args, name):
    n_grid = len(grid)
    if job is None:
        return pl.pallas_call(body, out_shape=out_shape, grid=grid, in_specs=in_specs, out_specs=out_specs,
                              scratch_shapes=scratch_shapes, compiler_params=_params(n_grid), name=name)(*args)
    n_steps = 1
    for g in grid:
        n_steps *= g
    rows = job.src.shape[0]
    step_rows = rows // n_steps
    assert step_rows * n_steps == rows and step_rows % 16 == 0

    def flat_step(*idx):
        flat = idx[0]
        for g, i in zip(grid[1:], idx[1:]):
            flat = flat * g + i
        return flat

    n_in, n_out = len(in_specs), len(out_specs)

    def body_with_cast(*refs):
        ins, src_ref, rest = refs[:n_in], refs[n_in], refs[n_in + 1:]
        outs, dst_ref, scratch = rest[:n_out], rest[n_out], rest[n_out + 1:]
        body(*ins, *outs, *scratch)
        dst_ref[...] = src_ref[...].astype(BF16)

    return pl.pallas_call(
        body_with_cast,
        out_shape=tuple(out_shape) + (jax.ShapeDtypeStruct((rows, job.cols), BF16),),
        grid=grid,
        in_specs=list(in_specs) + [pl.BlockSpec((step_rows, job.cols), lambda *idx: (flat_step(*idx), job.col_block))],
        out_specs=tuple(out_specs) + (pl.BlockSpec((step_rows, job.cols), lambda *idx: (flat_step(*idx), 0)),),
        scratch_shapes=scratch_shapes,
        compiler_params=_params(n_grid),
        name=name,
    )(*args, job.src)


def _norm_matmul_body(x_ref, g_ref, w_ref, o_ref, xn_ref, *, scale, n_scaled_tiles, n_col_tiles):
    @functools.partial(_once_per_row_tile, n_col_tiles=n_col_tiles, col_axis=1)
    def _():
        xn_ref[...] = _rms_scale(x_ref[...].astype(F32), g_ref[...]).astype(BF16)

    acc = jnp.dot(xn_ref[...], w_ref[...], preferred_element_type=F32)
    if scale is not None:
        acc = acc * jnp.where(pl.program_id(1) < n_scaled_tiles, scale, 1.0)
    o_ref[...] = acc.astype(o_ref.dtype)


def norm_matmul(x, g, w, out_dtype, tm, tn, name, cast_job=None, scale=None, scaled_cols=0):
    m, d = x.shape
    n = w.shape[1]
    tm, tn = min(tm, m), min(tn, n)
    assert scaled_cols % tn == 0
    out = _call_with_cast_job(
        functools.partial(_norm_matmul_body, scale=scale, n_scaled_tiles=scaled_cols // tn, n_col_tiles=n // tn),
        cast_job,
        (m // tm, n // tn),
        [pl.BlockSpec((tm, d), lambda i, j: (i, 0)),
         pl.BlockSpec((1, d), lambda i, j: (0, 0)),
         _weight_spec((d, tn), lambda i, j: (0, j), resident=tn == n)],
        (pl.BlockSpec((tm, tn), lambda i, j: (i, j)),),
        (jax.ShapeDtypeStruct((m, n), out_dtype),),
        [pltpu.VMEM((tm, d), BF16)],
        (x, g.reshape(1, d), w), name)
    return out[0] if cast_job is None else out


def _build_band_bias(g_ref, bias_ref, tq, tk):
    row = lax.broadcasted_iota(I32, (tq, tk), 0)
    col = lax.broadcasted_iota(I32, (tq, tk), 1)
    rel_chunk = col // CHUNK - N_LEFT_CHUNKS - row // CHUNK
    in_band = (rel_chunk <= 0) & (rel_chunk >= -N_LEFT_CHUNKS)
    lo = BIAS_SPAN - ATT_WINDOW - 2 * REL_MAX
    for h in range(N_HEADS_A):
        base = jnp.broadcast_to(g_ref[h:h + 1, :], (tq, BIAS_SPAN))
        rolled = pltpu.roll(base, 0, 1, stride=1, stride_axis=0)
        bias_ref[h] = jnp.where(in_band, rolled[:, lo:lo + tk], NEG_INF)


def _band_attn_scratch(tq, tk):
    return [pltpu.VMEM((N_HEADS_A, tq, tk), F32)]


def _band_attn_body(*refs, n_kv, tq, kw, dynamic_first):
    g_ref, q_ref = refs[0], refs[1]
    k_refs = refs[2:2 + n_kv]
    v_refs = refs[2 + n_kv:2 + 2 * n_kv]
    o_ref, bias_ref = refs[2 + 2 * n_kv], refs[3 + 2 * n_kv]
    tk = sum(kw)
    first = (pl.program_id(0) == 0) & (pl.program_id(1) == 0)

    @pl.when(first)
    def _():
        _build_band_bias(g_ref, bias_ref, tq, tk)

    def all_heads(mask_missing_keys):
        if mask_missing_keys:
            col = lax.broadcasted_iota(I32, (tq, tk), 1)
            exists = col >= ATT_WINDOW - pl.program_id(1) * tq
        for h in range(N_HEADS_A):
            sl = slice(h * HEAD_DIM_A, (h + 1) * HEAD_DIM_A)
            q = q_ref[:, sl]
            s = jnp.concatenate(
                [lax.dot_general(q, _head_cols(k_ref, h, HEAD_DIM_A).astype(BF16), (((1,), (1,)), ((), ())),
                                 preferred_element_type=F32) for k_ref in k_refs], axis=1)
            s = s + bias_ref[h]
            if mask_missing_keys:
                s = jnp.where(exists, s, NEG_INF)
            p = jnp.exp2(s - jnp.max(s, axis=-1, keepdims=True))
            l = jnp.sum(p, axis=-1, keepdims=True)
            pb = p.astype(BF16)
            o = None
            off = 0
            for v_ref, w in zip(v_refs, kw):
                part = jnp.dot(pb[:, off:off + w], _head_cols(v_ref, h, HEAD_DIM_A).astype(BF16),
                               preferred_element_type=F32)
                o = part if o is None else o + part
                off += w
            o_ref[:, sl] = (o / l).astype(o_ref.dtype)

    if dynamic_first:
        some_missing = pl.program_id(1) * tq < ATT_WINDOW
        pl.when(some_missing)(lambda: all_heads(True))
        pl.when(jnp.logical_not(some_missing))(lambda: all_heads(False))
    else:
        all_heads(False)


def band_attention_prompt(z, bias_rows, tq, cast_job=None):
    b, s, _ = z.shape
    n_kv = ATT_WINDOW // tq + 1
    tk = n_kv * tq
    blk = (None, tq, A_WIDTH)

    def kv_spec(back, colblk):
        return pl.BlockSpec(blk, lambda bi, i: (bi, jnp.maximum(i - back, 0), colblk))

    in_specs = [pl.BlockSpec((N_HEADS_A, BIAS_SPAN), lambda bi, i: (0, 0)),
                pl.BlockSpec(blk, lambda bi, i: (bi, i, 0))]
    in_specs += [kv_spec(n_kv - 1 - j, 1) for j in range(n_kv)]
    in_specs += [kv_spec(n_kv - 1 - j, 2) for j in range(n_kv)]
    out = _call_with_cast_job(
        functools.partial(_band_attn_body, n_kv=n_kv, tq=tq, kw=(tq,) * n_kv, dynamic_first=True), cast_job,
        (b, s // tq), in_specs, (pl.BlockSpec(blk, lambda bi, i: (bi, i, 0)),),
        (jax.ShapeDtypeStruct((b, s, A_WIDTH), BF16),),
        _band_attn_scratch(tq, tk),
        (bias_rows,) + (z,) * (1 + 2 * n_kv), "band_attn_prompt")
    return out[0] if cast_job is None else out


def band_attention_sample(z, cache_k, cache_v, bias_rows):
    b, t, _ = z.shape
    w = cache_k.shape[1]
    assert w == ATT_WINDOW and PAST_LEN >= w and PAST_LEN % CHUNK == 0 and t <= CHUNK
    new_blk = (None, t, A_WIDTH)
    old_blk = (None, w, N_HEADS_A, HEAD_DIM_A)
    return pl.pallas_call(
        functools.partial(_band_attn_body, n_kv=2, tq=t, kw=(w, t), dynamic_first=False),
        out_shape=jax.ShapeDtypeStruct((b, t, A_WIDTH), BF16),
        grid=(b, 1),
        in_specs=[pl.BlockSpec((N_HEADS_A, BIAS_SPAN), lambda bi, i: (0, 0)),
                  pl.BlockSpec(new_blk, lambda bi, i: (bi, 0, 0)),
                  pl.BlockSpec(old_blk, lambda bi, i: (bi, 0, 0, 0)),
                  pl.BlockSpec(new_blk, lambda bi, i: (bi, 0, 1)),
                  pl.BlockSpec(old_blk, lambda bi, i: (bi, 0, 0, 0)),
                  pl.BlockSpec(new_blk, lambda bi, i: (bi, 0, 2))],
        out_specs=pl.BlockSpec(new_blk, lambda bi, i: (bi, 0, 0)),
        scratch_shapes=_band_attn_scratch(t, w + t),
        compiler_params=_params(2),
        name="band_attn_sample",
    )(bias_rows, z, cache_k, z, cache_v, z)


def _outproj_body(att_ref, u_ref, hist_ref, x_ref, wp_ref, ps_ref, w_ref, o_ref, cat_ref, *, pos0, tm,
                  zero_first_hist, n_col_tiles):
    i = pl.program_id(1)

    @functools.partial(_once_per_row_tile, n_col_tiles=n_col_tiles, col_axis=2)
    def _():
        cat_ref[:, :A_WIDTH] = att_ref[...]
        hist = hist_ref[...].astype(F32)
        if zero_first_hist:
            hist = jnp.where(i > 0, hist, 0.0)
        ext = jnp.concatenate([hist, u_ref[...].astype(F32)], axis=0)
        pos = pos0 + i * tm + lax.broadcasted_iota(I32, (tm, 1), 0)
        for g, win in enumerate(POOL_WINDOWS):
            sl = slice(g * POOL_GROUP, (g + 1) * POOL_GROUP)
            e = ext[:, sl]
            run, span = e, 1
            while span < win:
                run = run[span:] + run[:-span]
                span *= 2
            first = HIST_ROWS - win + 1
            wsum = run[first:first + tm]
            cnt = jnp.minimum(pos + 1, win).astype(F32)
            d = (wsum / cnt - e[HIST_ROWS:]).astype(BF16)
            pool = jnp.dot(d, wp_ref[g], preferred_element_type=F32) * ps_ref[:, sl]
            cat_ref[:, A_WIDTH + g * POOL_GROUP:A_WIDTH + (g + 1) * POOL_GROUP] = pool.astype(BF16)

    o_ref[...] = x_ref[...] + jnp.dot(cat_ref[...], w_ref[...], preferred_element_type=F32)


def outproj(att, u_src, u_colblk, hist_src, hist_map, x, pool_w, pool_scale, w_out, pos0, tm, tn,
            zero_first_hist, name, cast_job=None):
    b, s, d = x.shape
    tm = min(tm, s)
    out = _call_with_cast_job(
        functools.partial(_outproj_body, pos0=pos0, tm=tm, zero_first_hist=zero_first_hist, n_col_tiles=d // tn),
        cast_job,
        (b, s // tm, d // tn),
        [pl.BlockSpec((None, tm, A_WIDTH), lambda bi, i, j: (bi, i, 0)),
         pl.BlockSpec((None, tm, B_WIDTH), lambda bi, i, j: (bi, i, u_colblk)),
         pl.BlockSpec((None, HIST_ROWS, B_WIDTH), hist_map),
         pl.BlockSpec((None, tm, tn), lambda bi, i, j: (bi, i, j)),
         pl.BlockSpec((len(POOL_WINDOWS), POOL_GROUP, POOL_GROUP), lambda bi, i, j: (0, 0, 0)),
         pl.BlockSpec((1, B_WIDTH), lambda bi, i, j: (0, 0)),
         _weight_spec((A_WIDTH + B_WIDTH, tn), lambda bi, i, j: (0, j), resident=tn == d)],
        (pl.BlockSpec((None, tm, tn), lambda bi, i, j: (bi, i, j)),),
        (jax.ShapeDtypeStruct((b, s, d), F32),),
        [pltpu.VMEM((tm, A_WIDTH + B_WIDTH), BF16)],
        (att, u_src, hist_src, x, pool_w, pool_scale.reshape(1, B_WIDTH), w_out), name)
    return out[0] if cast_job is None else out


def _memattn_body(q_ref, mk_ref, mv_ref, x_ref, w_ref, o_ref, oh_ref, *, n_col_tiles):
    @functools.partial(_once_per_row_tile, n_col_tiles=n_col_tiles, col_axis=2)
    def _():
        for h in range(N_HEADS_MEM):
            sl = slice(h * HEAD_DIM_MEM, (h + 1) * HEAD_DIM_MEM)
            s = lax.dot_general(q_ref[:, sl], _head_cols(mk_ref, h, HEAD_DIM_MEM).astype(BF16),
                                (((1,), (1,)), ((), ())), preferred_element_type=F32)
            p = jnp.exp(s - jnp.max(s, axis=-1, keepdims=True))
            l = jnp.sum(p, axis=-1, keepdims=True)
            o = jnp.dot(p.astype(BF16), _head_cols(mv_ref, h, HEAD_DIM_MEM).astype(BF16),
                        preferred_element_type=F32) / l
            oh_ref[:, sl] = o.astype(BF16)

    o_ref[...] = x_ref[...] + jnp.dot(oh_ref[...], w_ref[...], preferred_element_type=F32)


def mem_attention(q, mk, mv, x, w_mo, tm, tn, name, cast_job=None):
    b, s, d = x.shape
    tm = min(tm, s)
    kv_spec = pl.BlockSpec((None,) + mk.shape[1:], lambda bi, i, j: (bi,) + (0,) * (mk.ndim - 1))
    out = _call_with_cast_job(
        functools.partial(_memattn_body, n_col_tiles=d // tn), cast_job, (b, s // tm, d // tn),
        [pl.BlockSpec((None, tm, d), lambda bi, i, j: (bi, i, 0)),
         kv_spec,
         kv_spec,
         pl.BlockSpec((None, tm, tn), lambda bi, i, j: (bi, i, j)),
         _weight_spec((d, tn), lambda bi, i, j: (0, j), resident=tn == d)],
        (pl.BlockSpec((None, tm, tn), lambda bi, i, j: (bi, i, j)),),
        (jax.ShapeDtypeStruct((b, s, d), F32),),
        [pltpu.VMEM((tm, d), BF16)],
        (q, mk, mv, x, w_mo), name)
    return out[0] if cast_job is None else out


def _router_body(x_ref, g_ref, w_ref, b_ref, cin_ref, xn_ref, ids_ref, gates_ref, rank_ref, cnt_ref, carry_ref, *,
                 tm):
    @pl.when(pl.program_id(0) == 0)
    def _():
        carry_ref[...] = cin_ref[...].astype(F32)

    xn = _rms_scale(x_ref[...], g_ref[...])
    xn_ref[...] = xn
    x_hi = xn.astype(BF16)
    x_lo = (xn - x_hi.astype(F32)).astype(BF16)
    w = w_ref[...]
    w_hi = w.astype(BF16)
    w_lo = (w - w_hi.astype(F32)).astype(BF16)
    prod = jnp.dot(jnp.concatenate([x_hi, x_lo], axis=0), jnp.concatenate([w_hi, w_lo], axis=1),
                   preferred_element_type=F32)
    logits = (prod[:tm, :N_EXPERTS] + prod[tm:, :N_EXPERTS] + prod[:tm, N_EXPERTS:]) + b_ref[...]

    lane = lax.broadcasted_iota(I32, (tm, N_EXPERTS), 1)
    kcol = lax.broadcasted_iota(I32, (tm, TOP_K), 1)
    work = logits
    multi_hot = jnp.zeros((tm, N_EXPERTS), F32)
    ids = jnp.zeros((tm, TOP_K), I32)
    tops = jnp.zeros((tm, TOP_K), F32)
    picks = []
    for k in range(TOP_K):
        best = jnp.max(work, axis=-1, keepdims=True)
        idx = jnp.min(jnp.where(work == best, lane, N_EXPERTS), axis=-1, keepdims=True)
        hit = lane == idx
        picks.append(hit)
        multi_hot = multi_hot + hit.astype(F32)
        ids = jnp.where(kcol == k, idx, ids)
        tops = jnp.where(kcol == k, best, tops)
        work = jnp.where(hit, -jnp.inf, work)
    e = jnp.exp(tops - tops[:, 0:1])
    gates_ref[...] = e / jnp.sum(e, axis=-1, keepdims=True)
    ids_ref[...] = ids

    r = lax.broadcasted_iota(I32, (tm, tm), 0)
    c = lax.broadcasted_iota(I32, (tm, tm), 1)
    earlier = (c < r).astype(BF16)
    before = jnp.dot(earlier, multi_hot.astype(BF16), preferred_element_type=F32) + carry_ref[...]
    rank = jnp.zeros((tm, TOP_K), F32)
    for k in range(TOP_K):
        rk = jnp.sum(jnp.where(picks[k], before, 0.0), axis=-1, keepdims=True)
        rank = jnp.where(kcol == k, rk, rank)
    rank_ref[...] = rank.astype(I32)
    total = carry_ref[...] + jnp.sum(multi_hot, axis=0, keepdims=True)
    carry_ref[...] = total
    cnt_ref[...] = total.astype(I32)


def router(x, g, w_router, b_router, counts_in, tm, name):
    n, d = x.shape
    tm = min(tm, n)
    row = lambda i: (i, 0)
    fixed = lambda i: (0, 0)
    return pl.pallas_call(
        functools.partial(_router_body, tm=tm),
        out_shape=(jax.ShapeDtypeStruct((n, d), F32),
                   jax.ShapeDtypeStruct((n, TOP_K), I32),
                   jax.ShapeDtypeStruct((n, TOP_K), F32),
                   jax.ShapeDtypeStruct((n, TOP_K), I32),
                   jax.ShapeDtypeStruct((1, N_EXPERTS), I32)),
        grid=(n // tm,),
        in_specs=[pl.BlockSpec((tm, d), row), pl.BlockSpec((1, d), fixed),
                  pl.BlockSpec((d, N_EXPERTS), fixed), pl.BlockSpec((1, N_EXPERTS), fixed),
                  pl.BlockSpec((1, N_EXPERTS), fixed)],
        out_specs=(pl.BlockSpec((tm, d), row), pl.BlockSpec((tm, TOP_K), row), pl.BlockSpec((tm, TOP_K), row),
                   pl.BlockSpec((tm, TOP_K), row), pl.BlockSpec((1, N_EXPERTS), fixed)),
        scratch_shapes=[pltpu.VMEM((1, N_EXPERTS), F32)],
        compiler_params=_params(1),
        name=name,
    )(x, g.reshape(1, d), w_router, b_router.reshape(1, N_EXPERTS), counts_in)


def _dest_body(pstart_ref, ids_ref, rank_ref, dest_ref):
    ids = ids_ref[...]
    dest = rank_ref[...]
    for e in range(N_EXPERTS):
        dest = dest + jnp.where(ids == e, pstart_ref[e], 0)
    dest_ref[...] = dest


def assignment_dest(pstart, ids, rank, name):
    n_assign = ids.size
    shape = (n_assign // LANES, LANES)
    spec = pl.BlockSpec(shape, lambda i, ps: (0, 0))
    dest = pl.pallas_call(
        _dest_body,
        out_shape=jax.ShapeDtypeStruct(shape, I32),
        grid_spec=pltpu.PrefetchScalarGridSpec(num_scalar_prefetch=1, grid=(1,), in_specs=[spec, spec],
                                               out_specs=spec),
        compiler_params=_params(1),
        name=name,
    )(pstart, ids.reshape(shape), rank.reshape(shape))
    return dest.reshape(n_assign)


def _dispatch_body(fill_ref, cnt_ref, dest_ref, xp_ref, xs_ref, wsrc_ref, xbuf_hbm, wdst_ref, zero_ref, sem, zsem, *,
                   n_p_tiles):
    wdst_ref[...] = wsrc_ref[...].astype(BF16)

    def scatter_tile(x_ref):
        n_tok = x_ref.shape[0]

        def issue(t, carry):
            for k in range(TOP_K):
                pltpu.make_async_copy(x_ref.at[pl.ds(t, 1)],
                                      xbuf_hbm.at[pl.ds(dest_ref[0, 0, t * TOP_K + k], 1)], sem).start()
            return carry

        lax.fori_loop(0, n_tok, issue, 0, unroll=8)
        for k in range(TOP_K):
            pltpu.make_async_copy(x_ref, xbuf_hbm.at[pl.ds(0, n_tok)], sem).wait()

    @pl.when(pl.program_id(0) < n_p_tiles)
    def _():
        scatter_tile(xp_ref)

    @pl.when(pl.program_id(0) >= n_p_tiles)
    def _():
        scatter_tile(xs_ref)

    @pl.when(pl.program_id(0) == 0)
    def _():
        zero_ref[...] = jnp.zeros_like(zero_ref)

        def pad_copy(dst):
            return pltpu.make_async_copy(zero_ref.at[pl.ds(0, 1)], xbuf_hbm.at[pl.ds(dst, 1)], zsem)

        def tail_copy(c):
            return pltpu.make_async_copy(zero_ref, xbuf_hbm.at[pl.ds(pl.multiple_of(c * ZERO_ROWS, ZERO_ROWS),
                                                                     ZERO_ROWS)], zsem)

        first_tail = (fill_ref[N_EXPERTS - 1] + cnt_ref[N_EXPERTS - 1]) // ZERO_ROWS
        n_tail = xbuf_hbm.shape[0] // ZERO_ROWS

        def tail_start(c, carry):
            tail_copy(c).start()
            return carry

        def tail_wait(c, carry):
            tail_copy(c).wait()
            return carry

        lax.fori_loop(first_tail, n_tail, tail_start, 0)
        lax.fori_loop(first_tail, n_tail, tail_wait, 0)

        def per_expert(e, carry):
            start = fill_ref[e]
            n_pad = cnt_ref[e]

            def one(r, c2):
                pad_copy(start + r).start()
                return c2

            lax.fori_loop(0, n_pad, one, 0)

            def one_wait(r, c2):
                pad_copy(start + r).wait()
                return c2

            lax.fori_loop(0, n_pad, one_wait, 0)
            return carry

        lax.fori_loop(0, N_EXPERTS, per_expert, 0)


def dispatch(xn_p, xn_s, dest_p, dest_s, pad_start, pad_count, n_rows, tc, w_rows, name):
    (n_p, d), n_s = xn_p.shape, xn_s.shape[0]
    n_p_tiles = n_p // tc
    assert n_p_tiles * tc == n_p and n_s <= tc and n_s % 8 == 0
    dest = jnp.concatenate([dest_p, jnp.pad(dest_s, (0, (tc - n_s) * TOP_K))]).reshape(n_p_tiles + 1, 1, tc * TOP_K)
    w_step_rows = w_rows.shape[0] // n_p_tiles
    assert w_step_rows * n_p_tiles == w_rows.shape[0] and w_step_rows % 16 == 0
    w_spec = pl.BlockSpec((w_step_rows, w_rows.shape[1]), lambda i, a, b: (jnp.minimum(i, n_p_tiles - 1), 0))
    return pl.pallas_call(
        functools.partial(_dispatch_body, n_p_tiles=n_p_tiles),
        out_shape=(jax.ShapeDtypeStruct((n_rows, d), xn_p.dtype), jax.ShapeDtypeStruct(w_rows.shape, BF16)),
        grid_spec=pltpu.PrefetchScalarGridSpec(
            num_scalar_prefetch=2,
            grid=(n_p_tiles + 1,),
            in_specs=[pl.BlockSpec((1, 1, tc * TOP_K), lambda i, a, b: (i, 0, 0), memory_space=pltpu.SMEM),
                      pl.BlockSpec((tc, d), lambda i, a, b: (jnp.minimum(i, n_p_tiles - 1), 0)),
                      pl.BlockSpec((n_s, d), lambda i, a, b: (0, 0)),
                      w_spec],
            out_specs=(pl.BlockSpec(memory_space=pl.ANY), w_spec),
            scratch_shapes=[pltpu.VMEM((ZERO_ROWS, d), xn_p.dtype), pltpu.SemaphoreType.DMA,
                            pltpu.SemaphoreType.DMA]),
        compiler_params=_params(1),
        name=name,
    )(pad_start, pad_count, dest, xn_p, xn_s, w_rows)


def _expert_body(be_ref, nu_ref, x_ref, wg_ref, wu_ref, bg_ref, bu_ref, wd_ref, bd_ref, o_ref):
    j = pl.program_id(1)

    @pl.when(pl.program_id(0) < nu_ref[0])
    def _():
        xb = x_ref[...].astype(BF16)
        gate = jnp.dot(xb, wg_ref[...], preferred_element_type=F32) + bg_ref[...]
        up = jnp.dot(xb, wu_ref[...], preferred_element_type=F32) + bu_ref[...]
        gate = jnp.minimum(gate, SWIGLU_LIMIT)
        up = jnp.clip(up, -SWIGLU_LIMIT, SWIGLU_LIMIT)
        act = (up + 1.0) * (gate * jax.nn.sigmoid(SWIGLU_ALPHA * gate))
        start = jnp.where(j == 0, jnp.broadcast_to(bd_ref[...], o_ref.shape), o_ref[...])
        o_ref[...] = start + jnp.dot(act.astype(BF16), wd_ref[...], preferred_element_type=F32)

    @pl.when((pl.program_id(0) >= nu_ref[0]) & (j == 0))
    def _():
        o_ref[...] = jnp.zeros_like(o_ref)


def expert_ffn(block_expert, n_used, xbuf, w_gate, w_up, b_gu, w_down, b_down, bs, tf, name):
    n_rows, d = xbuf.shape
    n_blocks = n_rows // bs
    nt = D_FF // tf

    def blk(b, nu):
        return jnp.minimum(b, nu[0] - 1)

    def ff(b, j, nu):
        return jnp.where(b < nu[0], j, nt - 1)

    return pl.pallas_call(
        _expert_body,
        out_shape=jax.ShapeDtypeStruct((n_rows, d), F32),
        grid_spec=pltpu.PrefetchScalarGridSpec(
            num_scalar_prefetch=2,
            grid=(n_blocks, nt),
            in_specs=[pl.BlockSpec((bs, d), lambda b, j, be, nu: (blk(b, nu), 0)),
                      pl.BlockSpec((None, d, tf), lambda b, j, be, nu: (be[blk(b, nu)], 0, ff(b, j, nu))),
                      pl.BlockSpec((None, d, tf), lambda b, j, be, nu: (be[blk(b, nu)], 0, ff(b, j, nu))),
                      pl.BlockSpec((None, 1, tf), lambda b, j, be, nu: (be[blk(b, nu)], 0, ff(b, j, nu))),
                      pl.BlockSpec((None, 1, tf), lambda b, j, be, nu: (be[blk(b, nu)], 0, nt + ff(b, j, nu))),
                      pl.BlockSpec((None, tf, d), lambda b, j, be, nu: (be[blk(b, nu)], ff(b, j, nu), 0)),
                      pl.BlockSpec((None, 1, d), lambda b, j, be, nu: (be[blk(b, nu)], 0, 0))],
            out_specs=pl.BlockSpec((bs, d), lambda b, j, be, nu: (b, 0)),
            scratch_shapes=[]),
        compiler_params=_params(2),
        name=name,
    )(block_expert, n_used, xbuf, w_gate, w_up, b_gu.reshape(N_EXPERTS, 1, 2 * D_FF),
      b_gu.reshape(N_EXPERTS, 1, 2 * D_FF), w_down, b_down.reshape(N_EXPERTS, 1, d))


def _combine_body(dest_ref, next_ref, gates_ref, h_ref, g_ref, ybuf_hbm, o_ref, buf_ref, sem, *, tc):
    i = pl.program_id(0)

    def issue_tile(idx_ref, s):
        def issue(t, carry):
            for k in range(TOP_K):
                pltpu.make_async_copy(ybuf_hbm.at[pl.ds(idx_ref[0, 0, t * TOP_K + k], 1)],
                                      buf_ref.at[s, k, pl.ds(t, 1)], sem.at[s]).start()
            return carry

        lax.fori_loop(0, tc, issue, 0, unroll=8)

    def step(slot):
        if slot == 0:
            @pl.when(i == 0)
            def _():
                issue_tile(dest_ref, 0)

        @pl.when(i + 1 < pl.num_programs(0))
        def _():
            issue_tile(next_ref, 1 - slot)

        for k in range(TOP_K):
            pltpu.make_async_copy(ybuf_hbm.at[pl.ds(0, tc)], buf_ref.at[slot, k], sem.at[slot]).wait()

        gates = gates_ref[...]
        y = h_ref[...]
        for k in range(TOP_K):
            y = y + buf_ref[slot, k] * gates[:, k:k + 1]
        o_ref[...] = _rms_scale(y, g_ref[...])

    pl.when(i % 2 == 0)(lambda: step(0))
    pl.when(i % 2 == 1)(lambda: step(1))


def combine(dest, gates, h, g_final, ybuf, tc, name):
    n, d = h.shape
    tc = min(tc, n)
    n_tiles = n // tc
    dest3 = dest.reshape(n_tiles, 1, tc * TOP_K)
    return pl.pallas_call(
        functools.partial(_combine_body, tc=tc),
        out_shape=jax.ShapeDtypeStruct((n, d), F32),
        grid=(n_tiles,),
        in_specs=[pl.BlockSpec((1, 1, tc * TOP_K), lambda i: (i, 0, 0), memory_space=pltpu.SMEM),
                  pl.BlockSpec((1, 1, tc * TOP_K), lambda i: (jnp.minimum(i + 1, n_tiles - 1), 0, 0),
                               memory_space=pltpu.SMEM),
                  pl.BlockSpec((tc, TOP_K), lambda i: (i, 0)),
                  pl.BlockSpec((tc, d), lambda i: (i, 0)),
                  pl.BlockSpec((1, d), lambda i: (0, 0)),
                  pl.BlockSpec(memory_space=pl.ANY)],
        out_specs=pl.BlockSpec((tc, d), lambda i: (i, 0)),
        scratch_shapes=[pltpu.VMEM((2, TOP_K, tc, d), F32), pltpu.SemaphoreType.DMA((2,))],
        compiler_params=_params(1),
        name=name,
    )(dest3, dest3, gates, h, g_final.reshape(1, d), ybuf)


def moe_and_final_norm(h_p, h_s, g_ffn, w_router, b_router, w_gate, w_up, b_gu, w_down, b_down, g_final, *, bs,
                       tf):
    n_p, n_s = h_p.shape[0], h_s.shape[0]
    n_assign = (n_p + n_s) * TOP_K
    n_blocks = -(-n_assign // bs) + N_EXPERTS
    no_counts = jnp.zeros((1, N_EXPERTS), I32)
    xn_p, ids_p, gates_p, rank_p, counts_p = router(h_p, g_ffn, w_router, b_router, no_counts, 512, "router_p")
    xn_s, ids_s, gates_s, rank_s, counts = router(h_s, g_ffn, w_router, b_router, counts_p, 256, "router_s")
    counts = counts.reshape(N_EXPERTS)
    padded = (counts + bs - 1) // bs * bs
    pend = jnp.cumsum(padded)
    pstart = pend - padded
    n_used = (pend[-1:] // bs).astype(I32)
    block_row0 = jnp.arange(n_blocks, dtype=I32) * bs
    block_expert = jnp.minimum(jnp.sum((pend[None, :] <= block_row0[:, None]).astype(I32), axis=1), N_EXPERTS - 1)
    dest = assignment_dest(pstart.astype(I32), jnp.concatenate([ids_p, ids_s]), jnp.concatenate([rank_p, rank_s]),
                           "dest")
    dest_p, dest_s = dest[:n_p * TOP_K], dest[n_p * TOP_K:]
    xbuf, w_down_b = dispatch(xn_p, xn_s, dest_p, dest_s, (pstart + counts).astype(I32),
                              (padded - counts).astype(I32), n_blocks * bs, min(512, n_p),
                              w_down.reshape(N_EXPERTS * D_FF, -1), "dispatch")
    ybuf = expert_ffn(block_expert, n_used, xbuf, w_gate, w_up, b_gu, w_down_b.reshape(w_down.shape), b_down, bs,
                      tf, "experts")
    y_p = combine(dest_p, gates_p, h_p, g_final, ybuf, 256, "combine_p")
    y_s = combine(dest_s, gates_s, h_s, g_final, ybuf, 128, "combine_s")
    return y_p, y_s


def _bias_rows(table):
    flipped = table[:, ::-1]
    left = BIAS_SPAN - 3 * REL_MAX
    right = BIAS_SPAN - left - (2 * REL_MAX + 1)
    return jnp.pad(flipped, ((0, 0), (left, right)), mode="edge").astype(F32)


def kernel(x_prompt, x_sample, cache_attn_k, cache_attn_v, state_pool, cache_mem_k, cache_mem_v, mem_prompt, g_mix,
           w_in, rel_table, pool_w, pool_scale, w_out, g_memkv, w_mk, w_mv, g_mem, w_mq, w_mo, g_ffn, w_router,
           b_router, w_gu, b_gu, w_down, b_down, g_final):
    bp, s, d = x_prompt.shape
    bsz, t, _ = x_sample.shape
    keep = min(ATT_WINDOW, s)

    def layer(a):
        assert a.shape[0] == 1
        return a.reshape(a.shape[1:])

    (g_mix, w_in, rel_table, pool_w, pool_scale, w_out, g_memkv, w_mk, w_mv, g_mem, w_mq, w_mo, g_ffn, w_router,
     b_router, w_gu, b_gu, w_down, b_down, cache_attn_k, cache_attn_v, state_pool, cache_mem_k, cache_mem_v) = map(
        layer, (g_mix, w_in, rel_table, pool_w, pool_scale, w_out, g_memkv, w_mk, w_mv, g_mem, w_mq, w_mo, g_ffn,
                w_router, b_router, w_gu, b_gu, w_down, b_down, cache_attn_k, cache_attn_v, state_pool, cache_mem_k,
                cache_mem_v))
    w_in_b = w_in.astype(BF16)
    w_out_b = w_out.astype(BF16)
    w_mk_b, w_mv_b = w_mk.astype(BF16), w_mv.astype(BF16)
    w_mq_b, w_mo_b = w_mq.astype(BF16), w_mo.astype(BF16)
    pool_w_b = pool_w.astype(BF16)
    bias_rows = _bias_rows(rel_table) * LOG2E
    w_gu_rows = w_gu.reshape(N_EXPERTS * d, 2 * D_FF)
    cast_gate, cast_up = CastJob(w_gu_rows, 0, D_FF), CastJob(w_gu_rows, 1, D_FF)

    att_scale, mem_scale = HEAD_DIM_A ** -0.5 * LOG2E, HEAD_DIM_MEM ** -0.5
    z = norm_matmul(x_prompt.reshape(bp * s, d), g_mix, w_in_b, BF16, 1024, 1024, "in_proj_p",
                    scale=att_scale, scaled_cols=A_WIDTH)
    z = z.reshape(bp, s, IN_WIDTH)
    att, w_gate_b = band_attention_prompt(z, bias_rows, 256, cast_gate)
    tm_o = 512
    h, w_up_b = outproj(att, z, 3, z, lambda bi, i, j: (bi, jnp.maximum(i * (tm_o // HIST_ROWS) - 1, 0), 3),
                        x_prompt, pool_w_b, pool_scale, w_out_b, 0, tm_o, d, True, "outproj_p", cast_up)
    mem2d = mem_prompt.reshape(bp * N_MEM, d)
    mk = norm_matmul(mem2d, g_memkv, w_mk_b, F32, 1024, 1024, "mem_k_p").reshape(bp, N_MEM, d)
    mv = norm_matmul(mem2d, g_memkv, w_mv_b, F32, 1024, 1024, "mem_v_p").reshape(bp, N_MEM, d)
    qm = norm_matmul(h.reshape(bp * s, d), g_mem, w_mq_b, BF16, 1024, d, "mem_q_p", scale=mem_scale,
                     scaled_cols=d).reshape(bp, s, d)
    h = mem_attention(qm, mk.astype(BF16), mv.astype(BF16), h, w_mo_b, 512, d, "mem_attn_p")
    w_gate_b = w_gate_b.reshape(N_EXPERTS, d, D_FF)
    w_up_b = w_up_b.reshape(N_EXPERTS, d, D_FF)

    new_k_p = z[:, s - keep:, A_WIDTH:2 * A_WIDTH].astype(F32).reshape(1, bp, keep, N_HEADS_A, HEAD_DIM_A)
    new_v_p = z[:, s - keep:, 2 * A_WIDTH:3 * A_WIDTH].astype(F32).reshape(1, bp, keep, N_HEADS_A, HEAD_DIM_A)
    new_pool_p = z[:, s - POOL_HIST:, 3 * A_WIDTH:].astype(F32)[None]
    new_mk_p = mk.reshape(1, bp, N_MEM, N_HEADS_MEM, HEAD_DIM_MEM)
    new_mv_p = mv.reshape(1, bp, N_MEM, N_HEADS_MEM, HEAD_DIM_MEM)

    zs = norm_matmul(x_sample.reshape(bsz * t, d), g_mix, w_in_b, BF16, 256, 1024, "in_proj_s", scale=att_scale,
                     scaled_cols=A_WIDTH)
    zs = zs.reshape(bsz, t, IN_WIDTH)
    att_s = band_attention_sample(zs, cache_attn_k, cache_attn_v, bias_rows)
    hist_s = jnp.pad(state_pool, ((0, 0), (HIST_ROWS - POOL_HIST, 0), (0, 0)))
    hs = outproj(att_s, zs, 3, hist_s, lambda bi, i, j: (bi, 0, 0), x_sample, pool_w_b, pool_scale, w_out_b,
                 PAST_LEN, t, 1024, False, "outproj_s")
    qs = norm_matmul(hs.reshape(bsz * t, d), g_mem, w_mq_b, BF16, 256, 1024, "mem_q_s", scale=mem_scale,
                     scaled_cols=d).reshape(bsz, t, d)
    hs = mem_attention(qs, cache_mem_k, cache_mem_v, hs, w_mo_b, t, 1024, "mem_attn_s")

    y_prompt, y_sample = moe_and_final_norm(h.reshape(bp * s, d), hs.reshape(bsz * t, d), g_ffn, w_router, b_router,
                                            w_gate_b, w_up_b, b_gu, w_down, b_down, g_final, bs=512, tf=1024)
    y_prompt = y_prompt.reshape(bp, s, d)
    y_sample = y_sample.reshape(bsz, t, d)

    new_k_s = zs[:, :, A_WIDTH:2 * A_WIDTH].astype(F32).reshape(1, bsz, t, N_HEADS_A, HEAD_DIM_A)
    new_v_s = zs[:, :, 2 * A_WIDTH:3 * A_WIDTH].astype(F32).reshape(1, bsz, t, N_HEADS_A, HEAD_DIM_A)
    us_ext = jnp.concatenate([state_pool, zs[:, :, 3 * A_WIDTH:].astype(F32)], axis=1)
    new_pool_s = us_ext[:, -POOL_HIST:][None]

    return (y_prompt, y_sample, new_k_p, new_v_p, new_pool_p, new_mk_p, new_mv_p, new_k_s, new_v_s, new_pool_s)
```

```python
import functools
from typing import NamedTuple, Optional

import jax
import jax.numpy as jnp
from jax import lax
from jax.experimental import pallas as pl
from jax.experimental.pallas import tpu as pltpu

F32 = jnp.float32
BF16 = jnp.bfloat16
I32 = jnp.int32

D_MODEL = 2048
PAST_LEN = 4096
CHUNK = 64
N_LEFT_CHUNKS = 8
ATT_WINDOW = N_LEFT_CHUNKS * CHUNK
A_WIDTH = 1024
N_HEADS_A = 8
HEAD_DIM_A = A_WIDTH // N_HEADS_A
REL_MAX = 128
B_WIDTH = 1024
POOL_WINDOWS = (2, 4, 8, 16)
POOL_GROUP = B_WIDTH // len(POOL_WINDOWS)
POOL_HIST = max(POOL_WINDOWS) - 1
HIST_ROWS = POOL_HIST + 1
IN_WIDTH = 3 * A_WIDTH + B_WIDTH
N_MEM = 256
N_HEADS_MEM = 4
HEAD_DIM_MEM = D_MODEL // N_HEADS_MEM
N_EXPERTS = 32
TOP_K = 4
D_FF = D_MODEL
SWIGLU_LIMIT = 7.0
SWIGLU_ALPHA = 1.702
EPS = 1e-5
NEG_INF = -1e30

V7X_VMEM_LIMIT_BYTES = 56 * 1024 * 1024
LANES = 128
BIAS_SPAN = 1024
ZERO_ROWS = 128
LOG2E = 1.4426950408889634
N_DMA_PRIORITIES = 2


def _params(n_grid):
    return pltpu.CompilerParams(dimension_semantics=("arbitrary",) * n_grid,
                                vmem_limit_bytes=V7X_VMEM_LIMIT_BYTES)


def _weight_spec(block_shape, index_map, resident):
    if resident:
        return pl.BlockSpec(block_shape, index_map, pipeline_mode=pl.Buffered(1))
    return pl.BlockSpec(block_shape, index_map)


def _once_per_row_tile(fn, n_col_tiles, col_axis):
    if n_col_tiles == 1:
        fn()
    else:
        pl.when(pl.program_id(col_axis) == 0)(fn)


def _head_cols(ref, h, width):
    if len(ref.shape) == 3:
        return ref[:, h, :]
    return ref[:, h * width:(h + 1) * width]


def _rms_scale(x, g):
    return x * lax.rsqrt(jnp.mean(x * x, axis=-1, keepdims=True) + EPS) * g


class CastJob(NamedTuple):
    src: jax.Array
    col_block: int
    cols: int


def _call_with_cast_job(body, job, grid, in_specs, out_specs, out_shape, scratch_shapes, args, name):
    n_grid = len(grid)
    if job is None:
        return pl.pallas_call(body, out_shape=out_shape, grid=grid, in_specs=in_specs, out_specs=out_specs,
                              scratch_shapes=scratch_shapes, compiler_params=_params(n_grid), name=name)(*args)
    n_steps = 1
    for g in grid:
        n_steps *= g
    rows = job.src.shape[0]
    step_rows = rows // n_steps
    assert step_rows * n_steps == rows and step_rows % 16 == 0

    def flat_step(*idx):
        flat = idx[0]
        for g, i in zip(grid[1:], idx[1:]):
            flat = flat * g + i
        return flat

    n_in, n_out = len(in_specs), len(out_specs)

    def body_with_cast(*refs):
        ins, src_ref, rest = refs[:n_in], refs[n_in], refs[n_in + 1:]
        outs, dst_ref, scratch = rest[:n_out], rest[n_out], rest[n_out + 1:]
        body(*ins, *outs, *scratch)
        dst_ref[...] = src_ref[...].astype(BF16)

    return pl.pallas_call(
        body_with_cast,
        out_shape=tuple(out_shape) + (jax.ShapeDtypeStruct((rows, job.cols), BF16),),
        grid=grid,
        in_specs=list(in_specs) + [pl.BlockSpec((step_rows, job.cols), lambda *idx: (flat_step(*idx), job.col_block))],
        out_specs=tuple(out_specs) + (pl.BlockSpec((step_rows, job.cols), lambda *idx: (flat_step(*idx), 0)),),
        scratch_shapes=scratch_shapes,
        compiler_params=_params(n_grid),
        name=name,
    )(*args, job.src)


def _norm_matmul_body(x_ref, g_ref, w_ref, o_ref, xn_ref, *, scale, n_scaled_tiles, n_col_tiles):
    @functools.partial(_once_per_row_tile, n_col_tiles=n_col_tiles, col_axis=1)
    def _():
        xn_ref[...] = _rms_scale(x_ref[...].astype(F32), g_ref[...]).astype(BF16)

    acc = jnp.dot(xn_ref[...], w_ref[...], preferred_element_type=F32)
    if scale is not None:
        acc = acc * jnp.where(pl.program_id(1) < n_scaled_tiles, scale, 1.0)
    o_ref[...] = acc.astype(o_ref.dtype)


def norm_matmul(x, g, w, out_dtype, tm, tn, name, cast_job=None, scale=None, scaled_cols=0):
    m, d = x.shape
    n = w.shape[1]
    tm, tn = min(tm, m), min(tn, n)
    assert scaled_cols % tn == 0
    out = _call_with_cast_job(
        functools.partial(_norm_matmul_body, scale=scale, n_scaled_tiles=scaled_cols // tn, n_col_tiles=n // tn),
        cast_job,
        (m // tm, n // tn),
        [pl.BlockSpec((tm, d), lambda i, j: (i, 0)),
         pl.BlockSpec((1, d), lambda i, j: (0, 0)),
         _weight_spec((d, tn), lambda i, j: (0, j), resident=tn == n)],
        (pl.BlockSpec((tm, tn), lambda i, j: (i, j)),),
        (jax.ShapeDtypeStruct((m, n), out_dtype),),
        [pltpu.VMEM((tm, d), BF16)],
        (x, g.reshape(1, d), w), name)
    return out[0] if cast_job is None else out


def _build_band_bias(g_ref, bias_ref, tq, tk):
    row = lax.broadcasted_iota(I32, (tq, tk), 0)
    col = lax.broadcasted_iota(I32, (tq, tk), 1)
    rel_chunk = col // CHUNK - N_LEFT_CHUNKS - row // CHUNK
    in_band = (rel_chunk <= 0) & (rel_chunk >= -N_LEFT_CHUNKS)
    lo = BIAS_SPAN - ATT_WINDOW - 2 * REL_MAX
    for h in range(N_HEADS_A):
        base = jnp.broadcast_to(g_ref[h:h + 1, :], (tq, BIAS_SPAN))
        rolled = pltpu.roll(base, 0, 1, stride=1, stride_axis=0)
        bias_ref[h] = jnp.where(in_band, rolled[:, lo:lo + tk], NEG_INF)


def _band_attn_scratch(tq, tk):
    return [pltpu.VMEM((N_HEADS_A, tq, tk), F32)]


def _band_attn_body(*refs, n_kv, tq, kw, dynamic_first):
    g_ref, q_ref = refs[0], refs[1]
    k_refs = refs[2:2 + n_kv]
    v_refs = refs[2 + n_kv:2 + 2 * n_kv]
    o_ref, bias_ref = refs[2 + 2 * n_kv], refs[3 + 2 * n_kv]
    tk = sum(kw)
    first = (pl.program_id(0) == 0) & (pl.program_id(1) == 0)

    @pl.when(first)
    def _():
        _build_band_bias(g_ref, bias_ref, tq, tk)

    def all_heads(mask_missing_keys):
        if mask_missing_keys:
            col = lax.broadcasted_iota(I32, (tq, tk), 1)
            exists = col >= ATT_WINDOW - pl.program_id(1) * tq
        for h in range(N_HEADS_A):
            sl = slice(h * HEAD_DIM_A, (h + 1) * HEAD_DIM_A)
            q = q_ref[:, sl]
            s = jnp.concatenate(
                [lax.dot_general(q, _head_cols(k_ref, h, HEAD_DIM_A).astype(BF16), (((1,), (1,)), ((), ())),
                                 preferred_element_type=F32) for k_ref in k_refs], axis=1)
            s = s + bias_ref[h]
            if mask_missing_keys:
                s = jnp.where(exists, s, NEG_INF)
            p = jnp.exp2(s - jnp.max(s, axis=-1, keepdims=True))
            l = jnp.sum(p, axis=-1, keepdims=True)
            pb = p.astype(BF16)
            o = None
            off = 0
            for v_ref, w in zip(v_refs, kw):
                part = jnp.dot(pb[:, off:off + w], _head_cols(v_ref, h, HEAD_DIM_A).astype(BF16),
                               preferred_element_type=F32)
                o = part if o is None else o + part
                off += w
            o_ref[:, sl] = (o / l).astype(o_ref.dtype)

    if dynamic_first:
        some_missing = pl.program_id(1) * tq < ATT_WINDOW
        pl.when(some_missing)(lambda: all_heads(True))
        pl.when(jnp.logical_not(some_missing))(lambda: all_heads(False))
    else:
        all_heads(False)


def band_attention_prompt(z, bias_rows, tq, cast_job=None):
    b, s, _ = z.shape
    n_kv = ATT_WINDOW // tq + 1
    tk = n_kv * tq
    blk = (None, tq, A_WIDTH)

    def kv_spec(back, colblk):
        return pl.BlockSpec(blk, lambda bi, i: (bi, jnp.maximum(i - back, 0), colblk))

    in_specs = [pl.BlockSpec((N_HEADS_A, BIAS_SPAN), lambda bi, i: (0, 0)),
                pl.BlockSpec(blk, lambda bi, i: (bi, i, 0))]
    in_specs += [kv_spec(n_kv - 1 - j, 1) for j in range(n_kv)]
    in_specs += [kv_spec(n_kv - 1 - j, 2) for j in range(n_kv)]
    out = _call_with_cast_job(
        functools.partial(_band_attn_body, n_kv=n_kv, tq=tq, kw=(tq,) * n_kv, dynamic_first=True), cast_job,
        (b, s // tq), in_specs, (pl.BlockSpec(blk, lambda bi, i: (bi, i, 0)),),
        (jax.ShapeDtypeStruct((b, s, A_WIDTH), BF16),),
        _band_attn_scratch(tq, tk),
        (bias_rows,) + (z,) * (1 + 2 * n_kv), "band_attn_prompt")
    return out[0] if cast_job is None else out


def band_attention_sample(z, cache_k, cache_v, bias_rows):
    b, t, _ = z.shape
    w = cache_k.shape[1]
    assert w == ATT_WINDOW and PAST_LEN >= w and PAST_LEN % CHUNK == 0 and t <= CHUNK
    new_blk = (None, t, A_WIDTH)
    old_blk = (None, w, N_HEADS_A, HEAD_DIM_A)
    return pl.pallas_call(
        functools.partial(_band_attn_body, n_kv=2, tq=t, kw=(w, t), dynamic_first=False),
        out_shape=jax.ShapeDtypeStruct((b, t, A_WIDTH), BF16),
        grid=(b, 1),
        in_specs=[pl.BlockSpec((N_HEADS_A, BIAS_SPAN), lambda bi, i: (0, 0)),
                  pl.BlockSpec(new_blk, lambda bi, i: (bi, 0, 0)),
                  pl.BlockSpec(old_blk, lambda bi, i: (bi, 0, 0, 0)),
                  pl.BlockSpec(new_blk, lambda bi, i: (bi, 0, 1)),
                  pl.BlockSpec(old_blk, lambda bi, i: (bi, 0, 0, 0)),
                  pl.BlockSpec(new_blk, lambda bi, i: (bi, 0, 2))],
        out_specs=pl.BlockSpec(new_blk, lambda bi, i: (bi, 0, 0)),
        scratch_shapes=_band_attn_scratch(t, w + t),
        compiler_params=_params(2),
        name="band_attn_sample",
    )(bias_rows, z, cache_k, z, cache_v, z)


def _outproj_body(att_ref, u_ref, hist_ref, x_ref, wp_ref, ps_ref, w_ref, o_ref, cat_ref, *, pos0, tm,
                  zero_first_hist, n_col_tiles):
    i = pl.program_id(1)

    @functools.partial(_once_per_row_tile, n_col_tiles=n_col_tiles, col_axis=2)
    def _():
        cat_ref[:, :A_WIDTH] = att_ref[...]
        hist = hist_ref[...].astype(F32)
        if zero_first_hist:
            hist = jnp.where(i > 0, hist, 0.0)
        ext = jnp.concatenate([hist, u_ref[...].astype(F32)], axis=0)
        pos = pos0 + i * tm + lax.broadcasted_iota(I32, (tm, 1), 0)
        for g, win in enumerate(POOL_WINDOWS):
            sl = slice(g * POOL_GROUP, (g + 1) * POOL_GROUP)
            e = ext[:, sl]
            run, span = e, 1
            while span < win:
                run = run[span:] + run[:-span]
                span *= 2
            first = HIST_ROWS - win + 1
            wsum = run[first:first + tm]
            cnt = jnp.minimum(pos + 1, win).astype(F32)
            d = (wsum / cnt - e[HIST_ROWS:]).astype(BF16)
            pool = jnp.dot(d, wp_ref[g], preferred_element_type=F32) * ps_ref[:, sl]
            cat_ref[:, A_WIDTH + g * POOL_GROUP:A_WIDTH + (g + 1) * POOL_GROUP] = pool.astype(BF16)

    o_ref[...] = x_ref[...] + jnp.dot(cat_ref[...], w_ref[...], preferred_element_type=F32)


def outproj(att, u_src, u_colblk, hist_src, hist_map, x, pool_w, pool_scale, w_out, pos0, tm, tn,
            zero_first_hist, name, cast_job=None):
    b, s, d = x.shape
    tm = min(tm, s)
    out = _call_with_cast_job(
        functools.partial(_outproj_body, pos0=pos0, tm=tm, zero_first_hist=zero_first_hist, n_col_tiles=d // tn),
        cast_job,
        (b, s // tm, d // tn),
        [pl.BlockSpec((None, tm, A_WIDTH), lambda bi, i, j: (bi, i, 0)),
         pl.BlockSpec((None, tm, B_WIDTH), lambda bi, i, j: (bi, i, u_colblk)),
         pl.BlockSpec((None, HIST_ROWS, B_WIDTH), hist_map),
         pl.BlockSpec((None, tm, tn), lambda bi, i, j: (bi, i, j)),
         pl.BlockSpec((len(POOL_WINDOWS), POOL_GROUP, POOL_GROUP), lambda bi, i, j: (0, 0, 0)),
         pl.BlockSpec((1, B_WIDTH), lambda bi, i, j: (0, 0)),
         _weight_spec((A_WIDTH + B_WIDTH, tn), lambda bi, i, j: (0, j), resident=tn == d)],
        (pl.BlockSpec((None, tm, tn), lambda bi, i, j: (bi, i, j)),),
        (jax.ShapeDtypeStruct((b, s, d), F32),),
        [pltpu.VMEM((tm, A_WIDTH + B_WIDTH), BF16)],
        (att, u_src, hist_src, x, pool_w, pool_scale.reshape(1, B_WIDTH), w_out), name)
    return out[0] if cast_job is None else out


def _memattn_body(q_ref, mk_ref, mv_ref, x_ref, w_ref, o_ref, oh_ref, *, n_col_tiles):
    @functools.partial(_once_per_row_tile, n_col_tiles=n_col_tiles, col_axis=2)
    def _():
        for h in range(N_HEADS_MEM):
            sl = slice(h * HEAD_DIM_MEM, (h + 1) * HEAD_DIM_MEM)
            s = lax.dot_general(q_ref[:, sl], _head_cols(mk_ref, h, HEAD_DIM_MEM).astype(BF16),
                                (((1,), (1,)), ((), ())), preferred_element_type=F32)
            p = jnp.exp(s - jnp.max(s, axis=-1, keepdims=True))
            l = jnp.sum(p, axis=-1, keepdims=True)
            o = jnp.dot(p.astype(BF16), _head_cols(mv_ref, h, HEAD_DIM_MEM).astype(BF16),
                        preferred_element_type=F32) / l
            oh_ref[:, sl] = o.astype(BF16)

    o_ref[...] = x_ref[...] + jnp.dot(oh_ref[...], w_ref[...], preferred_element_type=F32)


def mem_attention(q, mk, mv, x, w_mo, tm, tn, name, cast_job=None):
    b, s, d = x.shape
    tm = min(tm, s)
    kv_spec = pl.BlockSpec((None,) + mk.shape[1:], lambda bi, i, j: (bi,) + (0,) * (mk.ndim - 1))
    out = _call_with_cast_job(
        functools.partial(_memattn_body, n_col_tiles=d // tn), cast_job, (b, s // tm, d // tn),
        [pl.BlockSpec((None, tm, d), lambda bi, i, j: (bi, i, 0)),
         kv_spec,
         kv_spec,
         pl.BlockSpec((None, tm, tn), lambda bi, i, j: (bi, i, j)),
         _weight_spec((d, tn), lambda bi, i, j: (0, j), resident=tn == d)],
        (pl.BlockSpec((None, tm, tn), lambda bi, i, j: (bi, i, j)),),
        (jax.ShapeDtypeStruct((b, s, d), F32),),
        [pltpu.VMEM((tm, d), BF16)],
        (q, mk, mv, x, w_mo), name)
    return out[0] if cast_job is None else out


def _router_body(x_ref, g_ref, w_ref, b_ref, cin_ref, xn_ref, ids_ref, gates_ref, rank_ref, cnt_ref, carry_ref, *,
                 tm):
    @pl.when(pl.program_id(0) == 0)
    def _():
        carry_ref[...] = cin_ref[...].astype(F32)

    xn = _rms_scale(x_ref[...], g_ref[...])
    xn_ref[...] = xn
    x_hi = xn.astype(BF16)
    x_lo = (xn - x_hi.astype(F32)).astype(BF16)
    w = w_ref[...]
    w_hi = w.astype(BF16)
    w_lo = (w - w_hi.astype(F32)).astype(BF16)
    prod = jnp.dot(jnp.concatenate([x_hi, x_lo], axis=0), jnp.concatenate([w_hi, w_lo], axis=1),
                   preferred_element_type=F32)
    logits = (prod[:tm, :N_EXPERTS] + prod[tm:, :N_EXPERTS] + prod[:tm, N_EXPERTS:]) + b_ref[...]

    lane = lax.broadcasted_iota(I32, (tm, N_EXPERTS), 1)
    kcol = lax.broadcasted_iota(I32, (tm, TOP_K), 1)
    work = logits
    multi_hot = jnp.zeros((tm, N_EXPERTS), F32)
    ids = jnp.zeros((tm, TOP_K), I32)
    tops = jnp.zeros((tm, TOP_K), F32)
    picks = []
    for k in range(TOP_K):
        best = jnp.max(work, axis=-1, keepdims=True)
        idx = jnp.min(jnp.where(work == best, lane, N_EXPERTS), axis=-1, keepdims=True)
        hit = lane == idx
        picks.append(hit)
        multi_hot = multi_hot + hit.astype(F32)
        ids = jnp.where(kcol == k, idx, ids)
        tops = jnp.where(kcol == k, best, tops)
        work = jnp.where(hit, -jnp.inf, work)
    e = jnp.exp(tops - tops[:, 0:1])
    gates_ref[...] = e / jnp.sum(e, axis=-1, keepdims=True)
    ids_ref[...] = ids

    r = lax.broadcasted_iota(I32, (tm, tm), 0)
    c = lax.broadcasted_iota(I32, (tm, tm), 1)
    earlier = (c < r).astype(BF16)
    before = jnp.dot(earlier, multi_hot.astype(BF16), preferred_element_type=F32) + carry_ref[...]
    rank = jnp.zeros((tm, TOP_K), F32)
    for k in range(TOP_K):
        rk = jnp.sum(jnp.where(picks[k], before, 0.0), axis=-1, keepdims=True)
        rank = jnp.where(kcol == k, rk, rank)
    rank_ref[...] = rank.astype(I32)
    total = carry_ref[...] + jnp.sum(multi_hot, axis=0, keepdims=True)
    carry_ref[...] = total
    cnt_ref[...] = total.astype(I32)


def router(x, g, w_router, b_router, counts_in, tm, name):
    n, d = x.shape
    tm = min(tm, n)
    row = lambda i: (i, 0)
    fixed = lambda i: (0, 0)
    return pl.pallas_call(
        functools.partial(_router_body, tm=tm),
        out_shape=(jax.ShapeDtypeStruct((n, d), F32),
                   jax.ShapeDtypeStruct((n, TOP_K), I32),
                   jax.ShapeDtypeStruct((n, TOP_K), F32),
                   jax.ShapeDtypeStruct((n, TOP_K), I32),
                   jax.ShapeDtypeStruct((1, N_EXPERTS), I32)),
        grid=(n // tm,),
        in_specs=[pl.BlockSpec((tm, d), row), pl.BlockSpec((1, d), fixed),
                  pl.BlockSpec((d, N_EXPERTS), fixed), pl.BlockSpec((1, N_EXPERTS), fixed),
                  pl.BlockSpec((1, N_EXPERTS), fixed)],
        out_specs=(pl.BlockSpec((tm, d), row), pl.BlockSpec((tm, TOP_K), row), pl.BlockSpec((tm, TOP_K), row),
                   pl.BlockSpec((tm, TOP_K), row), pl.BlockSpec((1, N_EXPERTS), fixed)),
        scratch_shapes=[pltpu.VMEM((1, N_EXPERTS), F32)],
        compiler_params=_params(1),
        name=name,
    )(x, g.reshape(1, d), w_router, b_router.reshape(1, N_EXPERTS), counts_in)


def _dest_body(pstart_ref, ids_ref, rank_ref, dest_ref):
    ids = ids_ref[...]
    dest = rank_ref[...]
    for e in range(N_EXPERTS):
        dest = dest + jnp.where(ids == e, pstart_ref[e], 0)
    dest_ref[...] = dest


def assignment_dest(pstart, ids, rank, name):
    n_assign = ids.size
    shape = (n_assign // LANES, LANES)
    spec = pl.BlockSpec(shape, lambda i, ps: (0, 0))
    dest = pl.pallas_call(
        _dest_body,
        out_shape=jax.ShapeDtypeStruct(shape, I32),
        grid_spec=pltpu.PrefetchScalarGridSpec(num_scalar_prefetch=1, grid=(1,), in_specs=[spec, spec],
                                               out_specs=spec),
        compiler_params=_params(1),
        name=name,
    )(pstart, ids.reshape(shape), rank.reshape(shape))
    return dest.reshape(n_assign)


def _dispatch_body(fill_ref, cnt_ref, dest_ref, xp_ref, xs_ref, wsrc_ref, xbuf_hbm, wdst_ref, zero_ref, sem, zsem, *,
                   n_p_tiles):
    wdst_ref[...] = wsrc_ref[...].astype(BF16)

    def scatter_tile(x_ref):
        n_tok = x_ref.shape[0]

        def issue(t, carry):
            for k in range(TOP_K):
                pltpu.make_async_copy(x_ref.at[pl.ds(t, 1)],
                                      xbuf_hbm.at[pl.ds(dest_ref[0, 0, t * TOP_K + k], 1)],
                                      sem).start(priority=k % N_DMA_PRIORITIES)
            return carry

        lax.fori_loop(0, n_tok, issue, 0, unroll=8)
        for k in range(TOP_K):
            pltpu.make_async_copy(x_ref, xbuf_hbm.at[pl.ds(0, n_tok)], sem).wait()

    @pl.when(pl.program_id(0) < n_p_tiles)
    def _():
        scatter_tile(xp_ref)

    @pl.when(pl.program_id(0) >= n_p_tiles)
    def _():
        scatter_tile(xs_ref)

    @pl.when(pl.program_id(0) == 0)
    def _():
        zero_ref[...] = jnp.zeros_like(zero_ref)

        def pad_copy(dst):
            return pltpu.make_async_copy(zero_ref.at[pl.ds(0, 1)], xbuf_hbm.at[pl.ds(dst, 1)], zsem)

        def tail_copy(c):
            return pltpu.make_async_copy(zero_ref, xbuf_hbm.at[pl.ds(pl.multiple_of(c * ZERO_ROWS, ZERO_ROWS),
                                                                     ZERO_ROWS)], zsem)

        first_tail = (fill_ref[N_EXPERTS - 1] + cnt_ref[N_EXPERTS - 1]) // ZERO_ROWS
        n_tail = xbuf_hbm.shape[0] // ZERO_ROWS

        def tail_start(c, carry):
            tail_copy(c).start()
            return carry

        def tail_wait(c, carry):
            tail_copy(c).wait()
            return carry

        lax.fori_loop(first_tail, n_tail, tail_start, 0)
        lax.fori_loop(first_tail, n_tail, tail_wait, 0)

        def per_expert(e, carry):
            start = fill_ref[e]
            n_pad = cnt_ref[e]

            def one(r, c2):
                pad_copy(start + r).start()
                return c2

            lax.fori_loop(0, n_pad, one, 0)

            def one_wait(r, c2):
                pad_copy(start + r).wait()
                return c2

            lax.fori_loop(0, n_pad, one_wait, 0)
            return carry

        lax.fori_loop(0, N_EXPERTS, per_expert, 0)


def dispatch(xn_p, xn_s, dest_p, dest_s, pad_start, pad_count, n_rows, tc, w_rows, name):
    (n_p, d), n_s = xn_p.shape, xn_s.shape[0]
    n_p_tiles = n_p // tc
    assert n_p_tiles * tc == n_p and n_s <= tc and n_s % 8 == 0
    dest = jnp.concatenate([dest_p, jnp.pad(dest_s, (0, (tc - n_s) * TOP_K))]).reshape(n_p_tiles + 1, 1, tc * TOP_K)
    w_step_rows = w_rows.shape[0] // n_p_tiles
    assert w_step_rows * n_p_tiles == w_rows.shape[0] and w_step_rows % 16 == 0
    w_spec = pl.BlockSpec((w_step_rows, w_rows.shape[1]), lambda i, a, b: (jnp.minimum(i, n_p_tiles - 1), 0))
    return pl.pallas_call(
        functools.partial(_dispatch_body, n_p_tiles=n_p_tiles),
        out_shape=(jax.ShapeDtypeStruct((n_rows, d), xn_p.dtype), jax.ShapeDtypeStruct(w_rows.shape, BF16)),
        grid_spec=pltpu.PrefetchScalarGridSpec(
            num_scalar_prefetch=2,
            grid=(n_p_tiles + 1,),
            in_specs=[pl.BlockSpec((1, 1, tc * TOP_K), lambda i, a, b: (i, 0, 0), memory_space=pltpu.SMEM),
                      pl.BlockSpec((tc, d), lambda i, a, b: (jnp.minimum(i, n_p_tiles - 1), 0)),
                      pl.BlockSpec((n_s, d), lambda i, a, b: (0, 0)),
                      w_spec],
            out_specs=(pl.BlockSpec(memory_space=pl.ANY), w_spec),
            scratch_shapes=[pltpu.VMEM((ZERO_ROWS, d), xn_p.dtype), pltpu.SemaphoreType.DMA,
                            pltpu.SemaphoreType.DMA]),
        compiler_params=_params(1),
        name=name,
    )(pad_start, pad_count, dest, xn_p, xn_s, w_rows)


def _expert_body(be_ref, nu_ref, x_ref, wg_ref, wu_ref, bg_ref, bu_ref, wd_ref, bd_ref, o_ref):
    j = pl.program_id(1)

    @pl.when(pl.program_id(0) < nu_ref[0])
    def _():
        xb = x_ref[...].astype(BF16)
        gate = jnp.dot(xb, wg_ref[...], preferred_element_type=F32) + bg_ref[...]
        up = jnp.dot(xb, wu_ref[...], preferred_element_type=F32) + bu_ref[...]
        gate = jnp.minimum(gate, SWIGLU_LIMIT)
        up = jnp.clip(up, -SWIGLU_LIMIT, SWIGLU_LIMIT)
        act = (up + 1.0) * (gate * jax.nn.sigmoid(SWIGLU_ALPHA * gate))
        start = jnp.where(j == 0, jnp.broadcast_to(bd_ref[...], o_ref.shape), o_ref[...])
        o_ref[...] = start + jnp.dot(act.astype(BF16), wd_ref[...], preferred_element_type=F32)

    @pl.when((pl.program_id(0) >= nu_ref[0]) & (j == 0))
    def _():
        o_ref[...] = jnp.zeros_like(o_ref)


def expert_ffn(block_expert, n_used, xbuf, w_gate, w_up, b_gu, w_down, b_down, bs, tf, name):
    n_rows, d = xbuf.shape
    n_blocks = n_rows // bs
    nt = D_FF // tf

    def blk(b, nu):
        return jnp.minimum(b, nu[0] - 1)

    def ff(b, j, nu):
        return jnp.where(b < nu[0], j, nt - 1)

    return pl.pallas_call(
        _expert_body,
        out_shape=jax.ShapeDtypeStruct((n_rows, d), F32),
        grid_spec=pltpu.PrefetchScalarGridSpec(
            num_scalar_prefetch=2,
            grid=(n_blocks, nt),
            in_specs=[pl.BlockSpec((bs, d), lambda b, j, be, nu: (blk(b, nu), 0)),
                      pl.BlockSpec((None, d, tf), lambda b, j, be, nu: (be[blk(b, nu)], 0, ff(b, j, nu))),
                      pl.BlockSpec((None, d, tf), lambda b, j, be, nu: (be[blk(b, nu)], 0, ff(b, j, nu))),
                      pl.BlockSpec((None, 1, tf), lambda b, j, be, nu: (be[blk(b, nu)], 0, ff(b, j, nu))),
                      pl.BlockSpec((None, 1, tf), lambda b, j, be, nu: (be[blk(b, nu)], 0, nt + ff(b, j, nu))),
                      pl.BlockSpec((None, tf, d), lambda b, j, be, nu: (be[blk(b, nu)], ff(b, j, nu), 0)),
                      pl.BlockSpec((None, 1, d), lambda b, j, be, nu: (be[blk(b, nu)], 0, 0))],
            out_specs=pl.BlockSpec((bs, d), lambda b, j, be, nu: (b, 0)),
            scratch_shapes=[]),
        compiler_params=_params(2),
        name=name,
    )(block_expert, n_used, xbuf, w_gate, w_up, b_gu.reshape(N_EXPERTS, 1, 2 * D_FF),
      b_gu.reshape(N_EXPERTS, 1, 2 * D_FF), w_down, b_down.reshape(N_EXPERTS, 1, d))


def _combine_body(dest_ref, next_ref, gates_ref, h_ref, g_ref, ybuf_hbm, o_ref, buf_ref, sem, *, tc):
    i = pl.program_id(0)

    def issue_tile(idx_ref, s):
        def issue(t, carry):
            for k in range(TOP_K):
                pltpu.make_async_copy(ybuf_hbm.at[pl.ds(idx_ref[0, 0, t * TOP_K + k], 1)],
                                      buf_ref.at[s, k, pl.ds(t, 1)],
                                      sem.at[s]).start(priority=k % N_DMA_PRIORITIES)
            return carry

        lax.fori_loop(0, tc, issue, 0, unroll=8)

    def step(slot):
        if slot == 0:
            @pl.when(i == 0)
            def _():
                issue_tile(dest_ref, 0)

        @pl.when(i + 1 < pl.num_programs(0))
        def _():
            issue_tile(next_ref, 1 - slot)

        for k in range(TOP_K):
            pltpu.make_async_copy(ybuf_hbm.at[pl.ds(0, tc)], buf_ref.at[slot, k], sem.at[slot]).wait()

        gates = gates_ref[...]
        y = h_ref[...]
        for k in range(TOP_K):
            y = y + buf_ref[slot, k] * gates[:, k:k + 1]
        o_ref[...] = _rms_scale(y, g_ref[...])

    pl.when(i % 2 == 0)(lambda: step(0))
    pl.when(i % 2 == 1)(lambda: step(1))


def combine(dest, gates, h, g_final, ybuf, tc, name):
    n, d = h.shape
    tc = min(tc, n)
    n_tiles = n // tc
    dest3 = dest.reshape(n_tiles, 1, tc * TOP_K)
    return pl.pallas_call(
        functools.partial(_combine_body, tc=tc),
        out_shape=jax.ShapeDtypeStruct((n, d), F32),
        grid=(n_tiles,),
        in_specs=[pl.BlockSpec((1, 1, tc * TOP_K), lambda i: (i, 0, 0), memory_space=pltpu.SMEM),
                  pl.BlockSpec((1, 1, tc * TOP_K), lambda i: (jnp.minimum(i + 1, n_tiles - 1), 0, 0),
                               memory_space=pltpu.SMEM),
                  pl.BlockSpec((tc, TOP_K), lambda i: (i, 0)),
                  pl.BlockSpec((tc, d), lambda i: (i, 0)),
                  pl.BlockSpec((1, d), lambda i: (0, 0)),
                  pl.BlockSpec(memory_space=pl.ANY)],
        out_specs=pl.BlockSpec((tc, d), lambda i: (i, 0)),
        scratch_shapes=[pltpu.VMEM((2, TOP_K, tc, d), F32), pltpu.SemaphoreType.DMA((2,))],
        compiler_params=_params(1),
        name=name,
    )(dest3, dest3, gates, h, g_final.reshape(1, d), ybuf)


def moe_and_final_norm(h_p, h_s, g_ffn, w_router, b_router, w_gate, w_up, b_gu, w_down, b_down, g_final, *, bs,
                       tf):
    n_p, n_s = h_p.shape[0], h_s.shape[0]
    n_assign = (n_p + n_s) * TOP_K
    n_blocks = -(-n_assign // bs) + N_EXPERTS
    no_counts = jnp.zeros((1, N_EXPERTS), I32)
    xn_p, ids_p, gates_p, rank_p, counts_p = router(h_p, g_ffn, w_router, b_router, no_counts, 512, "router_p")
    xn_s, ids_s, gates_s, rank_s, counts = router(h_s, g_ffn, w_router, b_router, counts_p, 256, "router_s")
    counts = counts.reshape(N_EXPERTS)
    padded = (counts + bs - 1) // bs * bs
    pend = jnp.cumsum(padded)
    pstart = pend - padded
    n_used = (pend[-1:] // bs).astype(I32)
    block_row0 = jnp.arange(n_blocks, dtype=I32) * bs
    block_expert = jnp.minimum(jnp.sum((pend[None, :] <= block_row0[:, None]).astype(I32), axis=1), N_EXPERTS - 1)
    dest = assignment_dest(pstart.astype(I32), jnp.concatenate([ids_p, ids_s]), jnp.concatenate([rank_p, rank_s]),
                           "dest")
    dest_p, dest_s = dest[:n_p * TOP_K], dest[n_p * TOP_K:]
    xbuf, w_down_b = dispatch(xn_p, xn_s, dest_p, dest_s, (pstart + counts).astype(I32),
                              (padded - counts).astype(I32), n_blocks * bs, min(512, n_p),
                              w_down.reshape(N_EXPERTS * D_FF, -1), "dispatch")
    ybuf = expert_ffn(block_expert, n_used, xbuf, w_gate, w_up, b_gu, w_down_b.reshape(w_down.shape), b_down, bs,
                      tf, "experts")
    y_p = combine(dest_p, gates_p, h_p, g_final, ybuf, 256, "combine_p")
    y_s = combine(dest_s, gates_s, h_s, g_final, ybuf, 128, "combine_s")
    return y_p, y_s


def _bias_rows(table):
    flipped = table[:, ::-1]
    left = BIAS_SPAN - 3 * REL_MAX
    right = BIAS_SPAN - left - (2 * REL_MAX + 1)
    return jnp.pad(flipped, ((0, 0), (left, right)), mode="edge").astype(F32)


def kernel(x_prompt, x_sample, cache_attn_k, cache_attn_v, state_pool, cache_mem_k, cache_mem_v, mem_prompt, g_mix,
           w_in, rel_table, pool_w, pool_scale, w_out, g_memkv, w_mk, w_mv, g_mem, w_mq, w_mo, g_ffn, w_router,
           b_router, w_gu, b_gu, w_down, b_down, g_final):
    bp, s, d = x_prompt.shape
    bsz, t, _ = x_sample.shape
    keep = min(ATT_WINDOW, s)

    def layer(a):
        assert a.shape[0] == 1
        return a.reshape(a.shape[1:])

    (g_mix, w_in, rel_table, pool_w, pool_scale, w_out, g_memkv, w_mk, w_mv, g_mem, w_mq, w_mo, g_ffn, w_router,
     b_router, w_gu, b_gu, w_down, b_down, cache_attn_k, cache_attn_v, state_pool, cache_mem_k, cache_mem_v) = map(
        layer, (g_mix, w_in, rel_table, pool_w, pool_scale, w_out, g_memkv, w_mk, w_mv, g_mem, w_mq, w_mo, g_ffn,
                w_router, b_router, w_gu, b_gu, w_down, b_down, cache_attn_k, cache_attn_v, state_pool, cache_mem_k,
                cache_mem_v))
    w_in_b = w_in.astype(BF16)
    w_out_b = w_out.astype(BF16)
    w_mk_b, w_mv_b = w_mk.astype(BF16), w_mv.astype(BF16)
    w_mq_b, w_mo_b = w_mq.astype(BF16), w_mo.astype(BF16)
    pool_w_b = pool_w.astype(BF16)
    bias_rows = _bias_rows(rel_table) * LOG2E
    w_gu_rows = w_gu.reshape(N_EXPERTS * d, 2 * D_FF)
    cast_gate, cast_up = CastJob(w_gu_rows, 0, D_FF), CastJob(w_gu_rows, 1, D_FF)

    att_scale, mem_scale = HEAD_DIM_A ** -0.5 * LOG2E, HEAD_DIM_MEM ** -0.5
    z = norm_matmul(x_prompt.reshape(bp * s, d), g_mix, w_in_b, BF16, 1024, 1024, "in_proj_p",
                    scale=att_scale, scaled_cols=A_WIDTH)
    z = z.reshape(bp, s, IN_WIDTH)
    att, w_gate_b = band_attention_prompt(z, bias_rows, 256, cast_gate)
    tm_o = 512
    h, w_up_b = outproj(att, z, 3, z, lambda bi, i, j: (bi, jnp.maximum(i * (tm_o // HIST_ROWS) - 1, 0), 3),
                        x_prompt, pool_w_b, pool_scale, w_out_b, 0, tm_o, d, True, "outproj_p", cast_up)
    mem2d = mem_prompt.reshape(bp * N_MEM, d)
    mk = norm_matmul(mem2d, g_memkv, w_mk_b, F32, 1024, 1024, "mem_k_p").reshape(bp, N_MEM, d)
    mv = norm_matmul(mem2d, g_memkv, w_mv_b, F32, 1024, 1024, "mem_v_p").reshape(bp, N_MEM, d)
    qm = norm_matmul(h.reshape(bp * s, d), g_mem, w_mq_b, BF16, 1024, d, "mem_q_p", scale=mem_scale,
                     scaled_cols=d).reshape(bp, s, d)
    h = mem_attention(qm, mk.astype(BF16), mv.astype(BF16), h, w_mo_b, 512, d, "mem_attn_p")
    w_gate_b = w_gate_b.reshape(N_EXPERTS, d, D_FF)
    w_up_b = w_up_b.reshape(N_EXPERTS, d, D_FF)

    new_k_p = z[:, s - keep:, A_WIDTH:2 * A_WIDTH].astype(F32).reshape(1, bp, keep, N_HEADS_A, HEAD_DIM_A)
    new_v_p = z[:, s - keep:, 2 * A_WIDTH:3 * A_WIDTH].astype(F32).reshape(1, bp, keep, N_HEADS_A, HEAD_DIM_A)
    new_pool_p = z[:, s - POOL_HIST:, 3 * A_WIDTH:].astype(F32)[None]
    new_mk_p = mk.reshape(1, bp, N_MEM, N_HEADS_MEM, HEAD_DIM_MEM)
    new_mv_p = mv.reshape(1, bp, N_MEM, N_HEADS_MEM, HEAD_DIM_MEM)

    zs = norm_matmul(x_sample.reshape(bsz * t, d), g_mix, w_in_b, BF16, 256, 1024, "in_proj_s", scale=att_scale,
                     scaled_cols=A_WIDTH)
    zs = zs.reshape(bsz, t, IN_WIDTH)
    att_s = band_attention_sample(zs, cache_attn_k, cache_attn_v, bias_rows)
    hist_s = jnp.pad(state_pool, ((0, 0), (HIST_ROWS - POOL_HIST, 0), (0, 0)))
    hs = outproj(att_s, zs, 3, hist_s, lambda bi, i, j: (bi, 0, 0), x_sample, pool_w_b, pool_scale, w_out_b,
                 PAST_LEN, t, 1024, False, "outproj_s")
    qs = norm_matmul(hs.reshape(bsz * t, d), g_mem, w_mq_b, BF16, 256, 1024, "mem_q_s", scale=mem_scale,
                     scaled_cols=d).reshape(bsz, t, d)
    hs = mem_attention(qs, cache_mem_k, cache_mem_v, hs, w_mo_b, t, 1024, "mem_attn_s")

    y_prompt, y_sample = moe_and_final_norm(h.reshape(bp * s, d), hs.reshape(bsz * t, d), g_ffn, w_router, b_router,
                                            w_gate_b, w_up_b, b_gu, w_down, b_down, g_final, bs=512, tf=1024)
    y_prompt = y_prompt.reshape(bp, s, d)
    y_sample = y_sample.reshape(bsz, t, d)

    new_k_s = zs[:, :, A_WIDTH:2 * A_WIDTH].astype(F32).reshape(1, bsz, t, N_HEADS_A, HEAD_DIM_A)
    new_v_s = zs[:, :, 2 * A_WIDTH:3 * A_WIDTH].astype(F32).reshape(1, bsz, t, N_HEADS_A, HEAD_DIM_A)
    us_ext = jnp.concatenate([state_pool, zs[:, :, 3 * A_WIDTH:].astype(F32)], axis=1)
    new_pool_s = us_ext[:, -POOL_HIST:][None]

    return (y_prompt, y_sample, new_k_p, new_v_p, new_pool_p, new_mk_p, new_mv_p, new_k_s, new_v_s, new_pool_s)
```

```python
import functools
from typing import NamedTuple, Optional

import jax
import jax.numpy as jnp
from jax import lax
from jax.experimental import pallas as pl
from jax.experimental.pallas import tpu as pltpu

F32 = jnp.float32
BF16 = jnp.bfloat16
I32 = jnp.int32

D_MODEL = 2048
PAST_LEN = 4096
CHUNK = 64
N_LEFT_CHUNKS = 8
ATT_WINDOW = N_LEFT_CHUNKS * CHUNK
A_WIDTH = 1024
N_HEADS_A = 8
HEAD_DIM_A = A_WIDTH // N_HEADS_A
REL_MAX = 128
B_WIDTH = 1024
POOL_WINDOWS = (2, 4, 8, 16)
POOL_GROUP = B_WIDTH // len(POOL_WINDOWS)
POOL_HIST = max(POOL_WINDOWS) - 1
HIST_ROWS = POOL_HIST + 1
IN_WIDTH = 3 * A_WIDTH + B_WIDTH
N_MEM = 256
N_HEADS_MEM = 4
HEAD_DIM_MEM = D_MODEL // N_HEADS_MEM
N_EXPERTS = 32
TOP_K = 4
D_FF = D_MODEL
SWIGLU_LIMIT = 7.0
SWIGLU_ALPHA = 1.702
EPS = 1e-5
NEG_INF = -1e30

V7X_VMEM_LIMIT_BYTES = 56 * 1024 * 1024
LANES = 128
BIAS_SPAN = 1024
ZERO_ROWS = 128
LOG2E = 1.4426950408889634
N_DMA_PRIORITIES = 2
ROW_GROUP = 8


def _params(n_grid):
    return pltpu.CompilerParams(dimension_semantics=("arbitrary",) * n_grid,
                                vmem_limit_bytes=V7X_VMEM_LIMIT_BYTES)


def _weight_spec(block_shape, index_map, resident):
    if resident:
        return pl.BlockSpec(block_shape, index_map, pipeline_mode=pl.Buffered(1))
    return pl.BlockSpec(block_shape, index_map)


def _once_per_row_tile(fn, n_col_tiles, col_axis):
    if n_col_tiles == 1:
        fn()
    else:
        pl.when(pl.program_id(col_axis) == 0)(fn)


def _head_cols(ref, h, width):
    if len(ref.shape) == 3:
        return ref[:, h, :]
    return ref[:, h * width:(h + 1) * width]


def _rms_scale(x, g):
    return x * lax.rsqrt(jnp.mean(x * x, axis=-1, keepdims=True) + EPS) * g


class CastJob(NamedTuple):
    src: jax.Array
    col_block: int
    cols: int


def _call_with_cast_job(body, job, grid, in_specs, out_specs, out_shape, scratch_shapes, args, name):
    n_grid = len(grid)
    if job is None:
        return pl.pallas_call(body, out_shape=out_shape, grid=grid, in_specs=in_specs, out_specs=out_specs,
                              scratch_shapes=scratch_shapes, compiler_params=_params(n_grid), name=name)(*args)
    n_steps = 1
    for g in grid:
        n_steps *= g
    rows = job.src.shape[0]
    step_rows = rows // n_steps
    assert step_rows * n_steps == rows and step_rows % 16 == 0

    def flat_step(*idx):
        flat = idx[0]
        for g, i in zip(grid[1:], idx[1:]):
            flat = flat * g + i
        return flat

    n_in, n_out = len(in_specs), len(out_specs)

    def body_with_cast(*refs):
        ins, src_ref, rest = refs[:n_in], refs[n_in], refs[n_in + 1:]
        outs, dst_ref, scratch = rest[:n_out], rest[n_out], rest[n_out + 1:]
        body(*ins, *outs, *scratch)
        dst_ref[...] = src_ref[...].astype(BF16)

    return pl.pallas_call(
        body_with_cast,
        out_shape=tuple(out_shape) + (jax.ShapeDtypeStruct((rows, job.cols), BF16),),
        grid=grid,
        in_specs=list(in_specs) + [pl.BlockSpec((step_rows, job.cols), lambda *idx: (flat_step(*idx), job.col_block))],
        out_specs=tuple(out_specs) + (pl.BlockSpec((step_rows, job.cols), lambda *idx: (flat_step(*idx), 0)),),
        scratch_shapes=scratch_shapes,
        compiler_params=_params(n_grid),
        name=name,
    )(*args, job.src)


def _norm_matmul_body(x_ref, g_ref, w_ref, o_ref, xn_ref, *, scale, n_scaled_tiles, n_col_tiles):
    @functools.partial(_once_per_row_tile, n_col_tiles=n_col_tiles, col_axis=1)
    def _():
        xn_ref[...] = _rms_scale(x_ref[...].astype(F32), g_ref[...]).astype(BF16)

    acc = jnp.dot(xn_ref[...], w_ref[...], preferred_element_type=F32)
    if scale is not None:
        acc = acc * jnp.where(pl.program_id(1) < n_scaled_tiles, scale, 1.0)
    o_ref[...] = acc.astype(o_ref.dtype)


def norm_matmul(x, g, w, out_dtype, tm, tn, name, cast_job=None, scale=None, scaled_cols=0):
    m, d = x.shape
    n = w.shape[1]
    tm, tn = min(tm, m), min(tn, n)
    assert scaled_cols % tn == 0
    out = _call_with_cast_job(
        functools.partial(_norm_matmul_body, scale=scale, n_scaled_tiles=scaled_cols // tn, n_col_tiles=n // tn),
        cast_job,
        (m // tm, n // tn),
        [pl.BlockSpec((tm, d), lambda i, j: (i, 0)),
         pl.BlockSpec((1, d), lambda i, j: (0, 0)),
         _weight_spec((d, tn), lambda i, j: (0, j), resident=tn == n)],
        (pl.BlockSpec((tm, tn), lambda i, j: (i, j)),),
        (jax.ShapeDtypeStruct((m, n), out_dtype),),
        [pltpu.VMEM((tm, d), BF16)],
        (x, g.reshape(1, d), w), name)
    return out[0] if cast_job is None else out


def _build_band_bias(g_ref, bias_ref, tq, tk):
    row = lax.broadcasted_iota(I32, (tq, tk), 0)
    col = lax.broadcasted_iota(I32, (tq, tk), 1)
    rel_chunk = col // CHUNK - N_LEFT_CHUNKS - row // CHUNK
    in_band = (rel_chunk <= 0) & (rel_chunk >= -N_LEFT_CHUNKS)
    lo = BIAS_SPAN - ATT_WINDOW - 2 * REL_MAX
    for h in range(N_HEADS_A):
        base = jnp.broadcast_to(g_ref[h:h + 1, :], (tq, BIAS_SPAN))
        rolled = pltpu.roll(base, 0, 1, stride=1, stride_axis=0)
        bias_ref[h] = jnp.where(in_band, rolled[:, lo:lo + tk], NEG_INF)


def _band_attn_scratch(tq, tk):
    return [pltpu.VMEM((N_HEADS_A, tq, tk), F32)]


def _band_attn_body(*refs, n_kv, tq, kw, dynamic_first):
    g_ref, q_ref = refs[0], refs[1]
    k_refs = refs[2:2 + n_kv]
    v_refs = refs[2 + n_kv:2 + 2 * n_kv]
    o_ref, bias_ref = refs[2 + 2 * n_kv], refs[3 + 2 * n_kv]
    tk = sum(kw)
    first = (pl.program_id(0) == 0) & (pl.program_id(1) == 0)

    @pl.when(first)
    def _():
        _build_band_bias(g_ref, bias_ref, tq, tk)

    def all_heads(mask_missing_keys):
        if mask_missing_keys:
            col = lax.broadcasted_iota(I32, (tq, tk), 1)
            exists = col >= ATT_WINDOW - pl.program_id(1) * tq
        for h in range(N_HEADS_A):
            sl = slice(h * HEAD_DIM_A, (h + 1) * HEAD_DIM_A)
            q = q_ref[:, sl]
            s = jnp.concatenate(
                [lax.dot_general(q, _head_cols(k_ref, h, HEAD_DIM_A).astype(BF16), (((1,), (1,)), ((), ())),
                                 preferred_element_type=F32) for k_ref in k_refs], axis=1)
            s = s + bias_ref[h]
            if mask_missing_keys:
                s = jnp.where(exists, s, NEG_INF)
            p = jnp.exp2(s - jnp.max(s, axis=-1, keepdims=True))
            l = jnp.sum(p, axis=-1, keepdims=True)
            pb = p.astype(BF16)
            o = None
            off = 0
            for v_ref, w in zip(v_refs, kw):
                part = jnp.dot(pb[:, off:off + w], _head_cols(v_ref, h, HEAD_DIM_A).astype(BF16),
                               preferred_element_type=F32)
                o = part if o is None else o + part
                off += w
            o_ref[:, sl] = (o / l).astype(o_ref.dtype)

    if dynamic_first:
        some_missing = pl.program_id(1) * tq < ATT_WINDOW
        pl.when(some_missing)(lambda: all_heads(True))
        pl.when(jnp.logical_not(some_missing))(lambda: all_heads(False))
    else:
        all_heads(False)


def band_attention_prompt(z, bias_rows, tq, cast_job=None):
    b, s, _ = z.shape
    n_kv = ATT_WINDOW // tq + 1
    tk = n_kv * tq
    blk = (None, tq, A_WIDTH)

    def kv_spec(back, colblk):
        return pl.BlockSpec(blk, lambda bi, i: (bi, jnp.maximum(i - back, 0), colblk))

    in_specs = [pl.BlockSpec((N_HEADS_A, BIAS_SPAN), lambda bi, i: (0, 0)),
                pl.BlockSpec(blk, lambda bi, i: (bi, i, 0))]
    in_specs += [kv_spec(n_kv - 1 - j, 1) for j in range(n_kv)]
    in_specs += [kv_spec(n_kv - 1 - j, 2) for j in range(n_kv)]
    out = _call_with_cast_job(
        functools.partial(_band_attn_body, n_kv=n_kv, tq=tq, kw=(tq,) * n_kv, dynamic_first=True), cast_job,
        (b, s // tq), in_specs, (pl.BlockSpec(blk, lambda bi, i: (bi, i, 0)),),
        (jax.ShapeDtypeStruct((b, s, A_WIDTH), BF16),),
        _band_attn_scratch(tq, tk),
        (bias_rows,) + (z,) * (1 + 2 * n_kv), "band_attn_prompt")
    return out[0] if cast_job is None else out


def band_attention_sample(z, cache_k, cache_v, bias_rows):
    b, t, _ = z.shape
    w = cache_k.shape[1]
    assert w == ATT_WINDOW and PAST_LEN >= w and PAST_LEN % CHUNK == 0 and t <= CHUNK
    new_blk = (None, t, A_WIDTH)
    old_blk = (None, w, N_HEADS_A, HEAD_DIM_A)
    return pl.pallas_call(
        functools.partial(_band_attn_body, n_kv=2, tq=t, kw=(w, t), dynamic_first=False),
        out_shape=jax.ShapeDtypeStruct((b, t, A_WIDTH), BF16),
        grid=(b, 1),
        in_specs=[pl.BlockSpec((N_HEADS_A, BIAS_SPAN), lambda bi, i: (0, 0)),
                  pl.BlockSpec(new_blk, lambda bi, i: (bi, 0, 0)),
                  pl.BlockSpec(old_blk, lambda bi, i: (bi, 0, 0, 0)),
                  pl.BlockSpec(new_blk, lambda bi, i: (bi, 0, 1)),
                  pl.BlockSpec(old_blk, lambda bi, i: (bi, 0, 0, 0)),
                  pl.BlockSpec(new_blk, lambda bi, i: (bi, 0, 2))],
        out_specs=pl.BlockSpec(new_blk, lambda bi, i: (bi, 0, 0)),
        scratch_shapes=_band_attn_scratch(t, w + t),
        compiler_params=_params(2),
        name="band_attn_sample",
    )(bias_rows, z, cache_k, z, cache_v, z)


def _outproj_body(att_ref, u_ref, hist_ref, x_ref, wp_ref, ps_ref, w_ref, o_ref, cat_ref, *, pos0, tm,
                  zero_first_hist, n_col_tiles):
    i = pl.program_id(1)

    @functools.partial(_once_per_row_tile, n_col_tiles=n_col_tiles, col_axis=2)
    def _():
        cat_ref[:, :A_WIDTH] = att_ref[...]
        hist = hist_ref[...].astype(F32)
        if zero_first_hist:
            hist = jnp.where(i > 0, hist, 0.0)
        ext = jnp.concatenate([hist, u_ref[...].astype(F32)], axis=0)
        pos = pos0 + i * tm + lax.broadcasted_iota(I32, (tm, 1), 0)
        for g, win in enumerate(POOL_WINDOWS):
            sl = slice(g * POOL_GROUP, (g + 1) * POOL_GROUP)
            e = ext[:, sl]
            run, span = e, 1
            while span < win:
                run = run[span:] + run[:-span]
                span *= 2
            first = HIST_ROWS - win + 1
            wsum = run[first:first + tm]
            cnt = jnp.minimum(pos + 1, win).astype(F32)
            d = (wsum / cnt - e[HIST_ROWS:]).astype(BF16)
            pool = jnp.dot(d, wp_ref[g], preferred_element_type=F32) * ps_ref[:, sl]
            cat_ref[:, A_WIDTH + g * POOL_GROUP:A_WIDTH + (g + 1) * POOL_GROUP] = pool.astype(BF16)

    o_ref[...] = x_ref[...] + jnp.dot(cat_ref[...], w_ref[...], preferred_element_type=F32)


def outproj(att, u_src, u_colblk, hist_src, hist_map, x, pool_w, pool_scale, w_out, pos0, tm, tn,
            zero_first_hist, name, cast_job=None):
    b, s, d = x.shape
    tm = min(tm, s)
    out = _call_with_cast_job(
        functools.partial(_outproj_body, pos0=pos0, tm=tm, zero_first_hist=zero_first_hist, n_col_tiles=d // tn),
        cast_job,
        (b, s // tm, d // tn),
        [pl.BlockSpec((None, tm, A_WIDTH), lambda bi, i, j: (bi, i, 0)),
         pl.BlockSpec((None, tm, B_WIDTH), lambda bi, i, j: (bi, i, u_colblk)),
         pl.BlockSpec((None, HIST_ROWS, B_WIDTH), hist_map),
         pl.BlockSpec((None, tm, tn), lambda bi, i, j: (bi, i, j)),
         pl.BlockSpec((len(POOL_WINDOWS), POOL_GROUP, POOL_GROUP), lambda bi, i, j: (0, 0, 0)),
         pl.BlockSpec((1, B_WIDTH), lambda bi, i, j: (0, 0)),
         _weight_spec((A_WIDTH + B_WIDTH, tn), lambda bi, i, j: (0, j), resident=tn == d)],
        (pl.BlockSpec((None, tm, tn), lambda bi, i, j: (bi, i, j)),),
        (jax.ShapeDtypeStruct((b, s, d), F32),),
        [pltpu.VMEM((tm, A_WIDTH + B_WIDTH), BF16)],
        (att, u_src, hist_src, x, pool_w, pool_scale.reshape(1, B_WIDTH), w_out), name)
    return out[0] if cast_job is None else out


def _memattn_body(q_ref, mk_ref, mv_ref, x_ref, w_ref, o_ref, oh_ref, *, n_col_tiles):
    @functools.partial(_once_per_row_tile, n_col_tiles=n_col_tiles, col_axis=2)
    def _():
        for h in range(N_HEADS_MEM):
            sl = slice(h * HEAD_DIM_MEM, (h + 1) * HEAD_DIM_MEM)
            s = lax.dot_general(q_ref[:, sl], _head_cols(mk_ref, h, HEAD_DIM_MEM).astype(BF16),
                                (((1,), (1,)), ((), ())), preferred_element_type=F32)
            p = jnp.exp(s - jnp.max(s, axis=-1, keepdims=True))
            l = jnp.sum(p, axis=-1, keepdims=True)
            o = jnp.dot(p.astype(BF16), _head_cols(mv_ref, h, HEAD_DIM_MEM).astype(BF16),
                        preferred_element_type=F32) / l
            oh_ref[:, sl] = o.astype(BF16)

    o_ref[...] = x_ref[...] + jnp.dot(oh_ref[...], w_ref[...], preferred_element_type=F32)


def mem_attention(q, mk, mv, x, w_mo, tm, tn, name, cast_job=None):
    b, s, d = x.shape
    tm = min(tm, s)
    kv_spec = pl.BlockSpec((None,) + mk.shape[1:], lambda bi, i, j: (bi,) + (0,) * (mk.ndim - 1))
    out = _call_with_cast_job(
        functools.partial(_memattn_body, n_col_tiles=d // tn), cast_job, (b, s // tm, d // tn),
        [pl.BlockSpec((None, tm, d), lambda bi, i, j: (bi, i, 0)),
         kv_spec,
         kv_spec,
         pl.BlockSpec((None, tm, tn), lambda bi, i, j: (bi, i, j)),
         _weight_spec((d, tn), lambda bi, i, j: (0, j), resident=tn == d)],
        (pl.BlockSpec((None, tm, tn), lambda bi, i, j: (bi, i, j)),),
        (jax.ShapeDtypeStruct((b, s, d), F32),),
        [pltpu.VMEM((tm, d), BF16)],
        (q, mk, mv, x, w_mo), name)
    return out[0] if cast_job is None else out


def _router_body(x_ref, g_ref, w_ref, b_ref, cin_ref, xn_ref, ids_ref, gates_ref, rank_ref, cnt_ref, carry_ref, *,
                 tm):
    @pl.when(pl.program_id(0) == 0)
    def _():
        carry_ref[...] = cin_ref[...].astype(F32)

    xn = _rms_scale(x_ref[...], g_ref[...])
    xn_ref[...] = xn
    x_hi = xn.astype(BF16)
    x_lo = (xn - x_hi.astype(F32)).astype(BF16)
    w = w_ref[...]
    w_hi = w.astype(BF16)
    w_lo = (w - w_hi.astype(F32)).astype(BF16)
    prod = jnp.dot(jnp.concatenate([x_hi, x_lo], axis=0), jnp.concatenate([w_hi, w_lo], axis=1),
                   preferred_element_type=F32)
    logits = (prod[:tm, :N_EXPERTS] + prod[tm:, :N_EXPERTS] + prod[:tm, N_EXPERTS:]) + b_ref[...]

    lane = lax.broadcasted_iota(I32, (tm, N_EXPERTS), 1)
    kcol = lax.broadcasted_iota(I32, (tm, TOP_K), 1)
    work = logits
    multi_hot = jnp.zeros((tm, N_EXPERTS), F32)
    ids = jnp.zeros((tm, TOP_K), I32)
    tops = jnp.zeros((tm, TOP_K), F32)
    picks = []
    for k in range(TOP_K):
        best = jnp.max(work, axis=-1, keepdims=True)
        idx = jnp.min(jnp.where(work == best, lane, N_EXPERTS), axis=-1, keepdims=True)
        hit = lane == idx
        picks.append(hit)
        multi_hot = multi_hot + hit.astype(F32)
        ids = jnp.where(kcol == k, idx, ids)
        tops = jnp.where(kcol == k, best, tops)
        work = jnp.where(hit, -jnp.inf, work)
    e = jnp.exp(tops - tops[:, 0:1])
    gates_ref[...] = e / jnp.sum(e, axis=-1, keepdims=True)
    ids_ref[...] = ids

    r = lax.broadcasted_iota(I32, (tm, tm), 0)
    c = lax.broadcasted_iota(I32, (tm, tm), 1)
    earlier = (c < r).astype(BF16)
    before = jnp.dot(earlier, multi_hot.astype(BF16), preferred_element_type=F32) + carry_ref[...]
    rank = jnp.zeros((tm, TOP_K), F32)
    for k in range(TOP_K):
        rk = jnp.sum(jnp.where(picks[k], before, 0.0), axis=-1, keepdims=True)
        rank = jnp.where(kcol == k, rk, rank)
    rank_ref[...] = rank.astype(I32)
    total = carry_ref[...] + jnp.sum(multi_hot, axis=0, keepdims=True)
    carry_ref[...] = total
    cnt_ref[...] = total.astype(I32)


def router(x, g, w_router, b_router, counts_in, tm, name):
    n, d = x.shape
    tm = min(tm, n)
    row = lambda i: (i, 0)
    fixed = lambda i: (0, 0)
    return pl.pallas_call(
        functools.partial(_router_body, tm=tm),
        out_shape=(jax.ShapeDtypeStruct((n, d), F32),
                   jax.ShapeDtypeStruct((n, TOP_K), I32),
                   jax.ShapeDtypeStruct((n, TOP_K), F32),
                   jax.ShapeDtypeStruct((n, TOP_K), I32),
                   jax.ShapeDtypeStruct((1, N_EXPERTS), I32)),
        grid=(n // tm,),
        in_specs=[pl.BlockSpec((tm, d), row), pl.BlockSpec((1, d), fixed),
                  pl.BlockSpec((d, N_EXPERTS), fixed), pl.BlockSpec((1, N_EXPERTS), fixed),
                  pl.BlockSpec((1, N_EXPERTS), fixed)],
        out_specs=(pl.BlockSpec((tm, d), row), pl.BlockSpec((tm, TOP_K), row), pl.BlockSpec((tm, TOP_K), row),
                   pl.BlockSpec((tm, TOP_K), row), pl.BlockSpec((1, N_EXPERTS), fixed)),
        scratch_shapes=[pltpu.VMEM((1, N_EXPERTS), F32)],
        compiler_params=_params(1),
        name=name,
    )(x, g.reshape(1, d), w_router, b_router.reshape(1, N_EXPERTS), counts_in)


def _dest_body(pstart_ref, ids_ref, rank_ref, dest_ref):
    ids = ids_ref[...]
    dest = rank_ref[...]
    for e in range(N_EXPERTS):
        dest = dest + jnp.where(ids == e, pstart_ref[e], 0)
    dest_ref[...] = dest


def assignment_dest(pstart, ids, rank, name):
    n_assign = ids.size
    shape = (n_assign // LANES, LANES)
    spec = pl.BlockSpec(shape, lambda i, ps: (0, 0))
    dest = pl.pallas_call(
        _dest_body,
        out_shape=jax.ShapeDtypeStruct(shape, I32),
        grid_spec=pltpu.PrefetchScalarGridSpec(num_scalar_prefetch=1, grid=(1,), in_specs=[spec, spec],
                                               out_specs=spec),
        compiler_params=_params(1),
        name=name,
    )(pstart, ids.reshape(shape), rank.reshape(shape))
    return dest.reshape(n_assign)


def _dispatch_body(fill_ref, cnt_ref, dest_ref, xp_ref, xs_ref, wsrc_ref, xbuf_hbm, wdst_ref, zero_ref, sem, zsem, *,
                   n_p_tiles):
    wdst_ref[...] = wsrc_ref[...].astype(BF16)

    def scatter_tile(x_ref):
        n_tok = x_ref.shape[0]

        def issue(t, carry):
            for k in range(TOP_K):
                pltpu.make_async_copy(x_ref.at[pl.ds(t, 1)],
                                      xbuf_hbm.at[pl.ds(dest_ref[0, 0, t * TOP_K + k], 1)],
                                      sem).start(priority=k % N_DMA_PRIORITIES)
            return carry

        lax.fori_loop(0, n_tok, issue, 0, unroll=8)
        for k in range(TOP_K):
            pltpu.make_async_copy(x_ref, xbuf_hbm.at[pl.ds(0, n_tok)], sem).wait()

    @pl.when(pl.program_id(0) < n_p_tiles)
    def _():
        scatter_tile(xp_ref)

    @pl.when(pl.program_id(0) >= n_p_tiles)
    def _():
        scatter_tile(xs_ref)

    @pl.when(pl.program_id(0) == 0)
    def _():
        zero_ref[...] = jnp.zeros_like(zero_ref)

        def pad_copy(dst):
            return pltpu.make_async_copy(zero_ref.at[pl.ds(0, 1)], xbuf_hbm.at[pl.ds(dst, 1)], zsem)

        def tail_copy(c):
            return pltpu.make_async_copy(zero_ref, xbuf_hbm.at[pl.ds(pl.multiple_of(c * ZERO_ROWS, ZERO_ROWS),
                                                                     ZERO_ROWS)], zsem)

        first_tail = (fill_ref[N_EXPERTS - 1] + cnt_ref[N_EXPERTS - 1]) // ZERO_ROWS
        n_tail = xbuf_hbm.shape[0] // ZERO_ROWS

        def tail_start(c, carry):
            tail_copy(c).start()
            return carry

        def tail_wait(c, carry):
            tail_copy(c).wait()
            return carry

        lax.fori_loop(first_tail, n_tail, tail_start, 0)
        lax.fori_loop(first_tail, n_tail, tail_wait, 0)

        def per_expert(e, carry):
            start = fill_ref[e]
            n_pad = cnt_ref[e]

            def one(r, c2):
                pad_copy(start + r).start()
                return c2

            lax.fori_loop(0, n_pad, one, 0)

            def one_wait(r, c2):
                pad_copy(start + r).wait()
                return c2

            lax.fori_loop(0, n_pad, one_wait, 0)
            return carry

        lax.fori_loop(0, N_EXPERTS, per_expert, 0)


def dispatch(xn_p, xn_s, dest_p, dest_s, pad_start, pad_count, n_rows, tc, w_rows, name):
    (n_p, d), n_s = xn_p.shape, xn_s.shape[0]
    n_p_tiles = n_p // tc
    assert n_p_tiles * tc == n_p and n_s <= tc and n_s % 8 == 0
    dest = jnp.concatenate([dest_p, jnp.pad(dest_s, (0, (tc - n_s) * TOP_K))]).reshape(n_p_tiles + 1, 1, tc * TOP_K)
    w_step_rows = w_rows.shape[0] // n_p_tiles
    assert w_step_rows * n_p_tiles == w_rows.shape[0] and w_step_rows % 16 == 0
    w_spec = pl.BlockSpec((w_step_rows, w_rows.shape[1]), lambda i, a, b: (jnp.minimum(i, n_p_tiles - 1), 0))
    return pl.pallas_call(
        functools.partial(_dispatch_body, n_p_tiles=n_p_tiles),
        out_shape=(jax.ShapeDtypeStruct((n_rows, d), xn_p.dtype), jax.ShapeDtypeStruct(w_rows.shape, BF16)),
        grid_spec=pltpu.PrefetchScalarGridSpec(
            num_scalar_prefetch=2,
            grid=(n_p_tiles + 1,),
            in_specs=[pl.BlockSpec((1, 1, tc * TOP_K), lambda i, a, b: (i, 0, 0), memory_space=pltpu.SMEM),
                      pl.BlockSpec((tc, d), lambda i, a, b: (jnp.minimum(i, n_p_tiles - 1), 0)),
                      pl.BlockSpec((n_s, d), lambda i, a, b: (0, 0)),
                      w_spec],
            out_specs=(pl.BlockSpec(memory_space=pl.ANY), w_spec),
            scratch_shapes=[pltpu.VMEM((ZERO_ROWS, d), xn_p.dtype), pltpu.SemaphoreType.DMA,
                            pltpu.SemaphoreType.DMA]),
        compiler_params=_params(1),
        name=name,
    )(pad_start, pad_count, dest, xn_p, xn_s, w_rows)


def _expert_body(be_ref, nu_ref, x_ref, wg_ref, wu_ref, bg_ref, bu_ref, wd_ref, bd_ref, o_ref):
    j = pl.program_id(1)

    @pl.when(pl.program_id(0) < nu_ref[0])
    def _():
        xb = x_ref[...].astype(BF16)
        gate = jnp.dot(xb, wg_ref[...], preferred_element_type=F32) + bg_ref[...]
        up = jnp.dot(xb, wu_ref[...], preferred_element_type=F32) + bu_ref[...]
        gate = jnp.minimum(gate, SWIGLU_LIMIT)
        up = jnp.clip(up, -SWIGLU_LIMIT, SWIGLU_LIMIT)
        act = (up + 1.0) * (gate * jax.nn.sigmoid(SWIGLU_ALPHA * gate))
        start = jnp.where(j == 0, jnp.broadcast_to(bd_ref[...], o_ref.shape), o_ref[...])
        o_ref[...] = start + jnp.dot(act.astype(BF16), wd_ref[...], preferred_element_type=F32)

    @pl.when((pl.program_id(0) >= nu_ref[0]) & (j == 0))
    def _():
        o_ref[...] = jnp.zeros_like(o_ref)


def expert_ffn(block_expert, n_used, xbuf, w_gate, w_up, b_gu, w_down, b_down, bs, tf, name):
    n_rows, d = xbuf.shape
    n_blocks = n_rows // bs
    nt = D_FF // tf

    def blk(b, nu):
        return jnp.minimum(b, nu[0] - 1)

    def ff(b, j, nu):
        return jnp.where(b < nu[0], j, nt - 1)

    return pl.pallas_call(
        _expert_body,
        out_shape=jax.ShapeDtypeStruct((n_rows, d), F32),
        grid_spec=pltpu.PrefetchScalarGridSpec(
            num_scalar_prefetch=2,
            grid=(n_blocks, nt),
            in_specs=[pl.BlockSpec((bs, d), lambda b, j, be, nu: (blk(b, nu), 0)),
                      pl.BlockSpec((None, d, tf), lambda b, j, be, nu: (be[blk(b, nu)], 0, ff(b, j, nu))),
                      pl.BlockSpec((None, d, tf), lambda b, j, be, nu: (be[blk(b, nu)], 0, ff(b, j, nu))),
                      pl.BlockSpec((None, 1, tf), lambda b, j, be, nu: (be[blk(b, nu)], 0, ff(b, j, nu))),
                      pl.BlockSpec((None, 1, tf), lambda b, j, be, nu: (be[blk(b, nu)], 0, nt + ff(b, j, nu))),
                      pl.BlockSpec((None, tf, d), lambda b, j, be, nu: (be[blk(b, nu)], ff(b, j, nu), 0)),
                      pl.BlockSpec((None, 1, d), lambda b, j, be, nu: (be[blk(b, nu)], 0, 0))],
            out_specs=pl.BlockSpec((bs, d), lambda b, j, be, nu: (b, 0)),
            scratch_shapes=[]),
        compiler_params=_params(2),
        name=name,
    )(block_expert, n_used, xbuf, w_gate, w_up, b_gu.reshape(N_EXPERTS, 1, 2 * D_FF),
      b_gu.reshape(N_EXPERTS, 1, 2 * D_FF), w_down, b_down.reshape(N_EXPERTS, 1, d))


def _combine_body(dest_ref, next_ref, gates_ref, h_ref, g_ref, ybuf_hbm, o_ref, buf0_ref, buf1_ref, sem, *, tc):
    i = pl.program_id(0)
    bufs = (buf0_ref, buf1_ref)
    n_groups = tc // ROW_GROUP

    def issue_rows(idx_ref, s, t0):
        for u in range(ROW_GROUP):
            for k in range(TOP_K):
                pltpu.make_async_copy(ybuf_hbm.at[pl.ds(idx_ref[0, 0, (t0 + u) * TOP_K + k], 1)],
                                      bufs[s].at[k, pl.ds(t0 + u, 1)],
                                      sem.at[s]).start(priority=k % N_DMA_PRIORITIES)

    def reduce_rows(slot, t0):
        rows = pl.ds(t0, ROW_GROUP)
        gates = gates_ref[rows, :]
        y = h_ref[rows, :]
        for k in range(TOP_K):
            y = y + bufs[slot][k, rows, :] * gates[:, k:k + 1]
        o_ref[rows, :] = _rms_scale(y, g_ref[...])

    def for_each_group(fn):
        def body(j, carry):
            fn(pl.multiple_of(j * ROW_GROUP, ROW_GROUP))
            return carry

        lax.fori_loop(0, n_groups, body, 0, unroll=4)

    def step(slot):
        if slot == 0:
            @pl.when(i == 0)
            def _():
                for_each_group(lambda t0: issue_rows(dest_ref, 0, t0))

        for k in range(TOP_K):
            pltpu.make_async_copy(ybuf_hbm.at[pl.ds(0, tc)], bufs[slot].at[k], sem.at[slot]).wait()

        has_next = i + 1 < pl.num_programs(0)

        @pl.when(has_next)
        def _():
            def issue_and_reduce(t0):
                issue_rows(next_ref, 1 - slot, t0)
                reduce_rows(slot, t0)

            for_each_group(issue_and_reduce)

        @pl.when(jnp.logical_not(has_next))
        def _():
            for_each_group(lambda t0: reduce_rows(slot, t0))

    pl.when(i % 2 == 0)(lambda: step(0))
    pl.when(i % 2 == 1)(lambda: step(1))


def combine(dest, gates, h, g_final, ybuf, tc, name):
    n, d = h.shape
    tc = min(tc, n)
    n_tiles = n // tc
    dest3 = dest.reshape(n_tiles, 1, tc * TOP_K)
    return pl.pallas_call(
        functools.partial(_combine_body, tc=tc),
        out_shape=jax.ShapeDtypeStruct((n, d), F32),
        grid=(n_tiles,),
        in_specs=[pl.BlockSpec((1, 1, tc * TOP_K), lambda i: (i, 0, 0), memory_space=pltpu.SMEM),
                  pl.BlockSpec((1, 1, tc * TOP_K), lambda i: (jnp.minimum(i + 1, n_tiles - 1), 0, 0),
                               memory_space=pltpu.SMEM),
                  pl.BlockSpec((tc, TOP_K), lambda i: (i, 0)),
                  pl.BlockSpec((tc, d), lambda i: (i, 0)),
                  pl.BlockSpec((1, d), lambda i: (0, 0)),
                  pl.BlockSpec(memory_space=pl.ANY)],
        out_specs=pl.BlockSpec((tc, d), lambda i: (i, 0)),
        scratch_shapes=[pltpu.VMEM((TOP_K, tc, d), F32), pltpu.VMEM((TOP_K, tc, d), F32),
                        pltpu.SemaphoreType.DMA((2,))],
        compiler_params=_params(1),
        name=name,
    )(dest3, dest3, gates, h, g_final.reshape(1, d), ybuf)


def moe_and_final_norm(h_p, h_s, g_ffn, w_router, b_router, w_gate, w_up, b_gu, w_down, b_down, g_final, *, bs,
                       tf):
    n_p, n_s = h_p.shape[0], h_s.shape[0]
    n_assign = (n_p + n_s) * TOP_K
    n_blocks = -(-n_assign // bs) + N_EXPERTS
    no_counts = jnp.zeros((1, N_EXPERTS), I32)
    xn_p, ids_p, gates_p, rank_p, counts_p = router(h_p, g_ffn, w_router, b_router, no_counts, 512, "router_p")
    xn_s, ids_s, gates_s, rank_s, counts = router(h_s, g_ffn, w_router, b_router, counts_p, 256, "router_s")
    counts = counts.reshape(N_EXPERTS)
    padded = (counts + bs - 1) // bs * bs
    pend = jnp.cumsum(padded)
    pstart = pend - padded
    n_used = (pend[-1:] // bs).astype(I32)
    block_row0 = jnp.arange(n_blocks, dtype=I32) * bs
    block_expert = jnp.minimum(jnp.sum((pend[None, :] <= block_row0[:, None]).astype(I32), axis=1), N_EXPERTS - 1)
    dest = assignment_dest(pstart.astype(I32), jnp.concatenate([ids_p, ids_s]), jnp.concatenate([rank_p, rank_s]),
                           "dest")
    dest_p, dest_s = dest[:n_p * TOP_K], dest[n_p * TOP_K:]
    xbuf, w_down_b = dispatch(xn_p, xn_s, dest_p, dest_s, (pstart + counts).astype(I32),
                              (padded - counts).astype(I32), n_blocks * bs, min(512, n_p),
                              w_down.reshape(N_EXPERTS * D_FF, -1), "dispatch")
    ybuf = expert_ffn(block_expert, n_used, xbuf, w_gate, w_up, b_gu, w_down_b.reshape(w_down.shape), b_down, bs,
                      tf, "experts")
    y_p = combine(dest_p, gates_p, h_p, g_final, ybuf, 256, "combine_p")
    y_s = combine(dest_s, gates_s, h_s, g_final, ybuf, 128, "combine_s")
    return y_p, y_s


def _bias_rows(table):
    flipped = table[:, ::-1]
    left = BIAS_SPAN - 3 * REL_MAX
    right = BIAS_SPAN - left - (2 * REL_MAX + 1)
    return jnp.pad(flipped, ((0, 0), (left, right)), mode="edge").astype(F32)


def kernel(x_prompt, x_sample, cache_attn_k, cache_attn_v, state_pool, cache_mem_k, cache_mem_v, mem_prompt, g_mix,
           w_in, rel_table, pool_w, pool_scale, w_out, g_memkv, w_mk, w_mv, g_mem, w_mq, w_mo, g_ffn, w_router,
           b_router, w_gu, b_gu, w_down, b_down, g_final):
    bp, s, d = x_prompt.shape
    bsz, t, _ = x_sample.shape
    keep = min(ATT_WINDOW, s)

    def layer(a):
        assert a.shape[0] == 1
        return a.reshape(a.shape[1:])

    (g_mix, w_in, rel_table, pool_w, pool_scale, w_out, g_memkv, w_mk, w_mv, g_mem, w_mq, w_mo, g_ffn, w_router,
     b_router, w_gu, b_gu, w_down, b_down, cache_attn_k, cache_attn_v, state_pool, cache_mem_k, cache_mem_v) = map(
        layer, (g_mix, w_in, rel_table, pool_w, pool_scale, w_out, g_memkv, w_mk, w_mv, g_mem, w_mq, w_mo, g_ffn,
                w_router, b_router, w_gu, b_gu, w_down, b_down, cache_attn_k, cache_attn_v, state_pool, cache_mem_k,
                cache_mem_v))
    w_in_b = w_in.astype(BF16)
    w_out_b = w_out.astype(BF16)
    w_mk_b, w_mv_b = w_mk.astype(BF16), w_mv.astype(BF16)
    w_mq_b, w_mo_b = w_mq.astype(BF16), w_mo.astype(BF16)
    pool_w_b = pool_w.astype(BF16)
    bias_rows = _bias_rows(rel_table) * LOG2E
    w_gu_rows = w_gu.reshape(N_EXPERTS * d, 2 * D_FF)
    cast_gate, cast_up = CastJob(w_gu_rows, 0, D_FF), CastJob(w_gu_rows, 1, D_FF)

    att_scale, mem_scale = HEAD_DIM_A ** -0.5 * LOG2E, HEAD_DIM_MEM ** -0.5
    z = norm_matmul(x_prompt.reshape(bp * s, d), g_mix, w_in_b, BF16, 1024, 1024, "in_proj_p",
                    scale=att_scale, scaled_cols=A_WIDTH)
    z = z.reshape(bp, s, IN_WIDTH)
    att, w_gate_b = band_attention_prompt(z, bias_rows, 256, cast_gate)
    tm_o = 512
    h, w_up_b = outproj(att, z, 3, z, lambda bi, i, j: (bi, jnp.maximum(i * (tm_o // HIST_ROWS) - 1, 0), 3),
                        x_prompt, pool_w_b, pool_scale, w_out_b, 0, tm_o, d, True, "outproj_p", cast_up)
    mem2d = mem_prompt.reshape(bp * N_MEM, d)
    mk = norm_matmul(mem2d, g_memkv, w_mk_b, F32, 1024, 1024, "mem_k_p").reshape(bp, N_MEM, d)
    mv = norm_matmul(mem2d, g_memkv, w_mv_b, F32, 1024, 1024, "mem_v_p").reshape(bp, N_MEM, d)
    qm = norm_matmul(h.reshape(bp * s, d), g_mem, w_mq_b, BF16, 1024, d, "mem_q_p", scale=mem_scale,
                     scaled_cols=d).reshape(bp, s, d)
    h = mem_attention(qm, mk.astype(BF16), mv.astype(BF16), h, w_mo_b, 512, d, "mem_attn_p")
    w_gate_b = w_gate_b.reshape(N_EXPERTS, d, D_FF)
    w_up_b = w_up_b.reshape(N_EXPERTS, d, D_FF)

    new_k_p = z[:, s - keep:, A_WIDTH:2 * A_WIDTH].astype(F32).reshape(1, bp, keep, N_HEADS_A, HEAD_DIM_A)
    new_v_p = z[:, s - keep:, 2 * A_WIDTH:3 * A_WIDTH].astype(F32).reshape(1, bp, keep, N_HEADS_A, HEAD_DIM_A)
    new_pool_p = z[:, s - POOL_HIST:, 3 * A_WIDTH:].astype(F32)[None]
    new_mk_p = mk.reshape(1, bp, N_MEM, N_HEADS_MEM, HEAD_DIM_MEM)
    new_mv_p = mv.reshape(1, bp, N_MEM, N_HEADS_MEM, HEAD_DIM_MEM)

    zs = norm_matmul(x_sample.reshape(bsz * t, d), g_mix, w_in_b, BF16, 256, 1024, "in_proj_s", scale=att_scale,
                     scaled_cols=A_WIDTH)
    zs = zs.reshape(bsz, t, IN_WIDTH)
    att_s = band_attention_sample(zs, cache_attn_k, cache_attn_v, bias_rows)
    hist_s = jnp.pad(state_pool, ((0, 0), (HIST_ROWS - POOL_HIST, 0), (0, 0)))
    hs = outproj(att_s, zs, 3, hist_s, lambda bi, i, j: (bi, 0, 0), x_sample, pool_w_b, pool_scale, w_out_b,
                 PAST_LEN, t, 1024, False, "outproj_s")
    qs = norm_matmul(hs.reshape(bsz * t, d), g_mem, w_mq_b, BF16, 256, 1024, "mem_q_s", scale=mem_scale,
                     scaled_cols=d).reshape(bsz, t, d)
    hs = mem_attention(qs, cache_mem_k, cache_mem_v, hs, w_mo_b, t, 1024, "mem_attn_s")

    y_prompt, y_sample = moe_and_final_norm(h.reshape(bp * s, d), hs.reshape(bsz * t, d), g_ffn, w_router, b_router,
                                            w_gate_b, w_up_b, b_gu, w_down, b_down, g_final, bs=512, tf=1024)
    y_prompt = y_prompt.reshape(bp, s, d)
    y_sample = y_sample.reshape(bsz, t, d)

    new_k_s = zs[:, :, A_WIDTH:2 * A_WIDTH].astype(F32).reshape(1, bsz, t, N_HEADS_A, HEAD_DIM_A)
    new_v_s = zs[:, :, 2 * A_WIDTH:3 * A_WIDTH].astype(F32).reshape(1, bsz, t, N_HEADS_A, HEAD_DIM_A)
    us_ext = jnp.concatenate([state_pool, zs[:, :, 3 * A_WIDTH:].astype(F32)], axis=1)
    new_pool_s = us_ext[:, -POOL_HIST:][None]

    return (y_prompt, y_sample, new_k_p, new_v_p, new_pool_p, new_mk_p, new_mv_p, new_k_s, new_v_s, new_pool_s)
```
